```python
import jax, jax.numpy as jnp
from jax import lax
import numpy as np

D_MODEL = 1024
BATCH = 4
SEQ = 4096
DEPTH = 4

N_A_LAYERS = DEPTH // 2
N_B_LAYERS = DEPTH - N_A_LAYERS
INNER_A = 2 * D_MODEL
HEADS_A = 4
HEAD_DIM_A = INNER_A // HEADS_A
QKV_BLOCK = 4
CONV_K = 4
CHUNK = 64
HEAD_DIM_B = 128
HEADS_B = D_MODEL // HEAD_DIM_B
INNER_B = HEADS_B * HEAD_DIM_B
GROUPS_B = ((128, 1), (512, 4), (2048, 16))
N_GROUPS_B = len(GROUPS_B)
BLOCK_B = 128
ALPHA = (2 * DEPTH) ** 0.25
BETA = (8 * DEPTH) ** -0.25
LN_EPS = 1e-5

kernel_name = 'yoco_mlstm_dilated_swa_deepnorm'


def layer_norm(x, g, b):
    xf = x.astype(jnp.float32)
    mu = xf.mean(-1, keepdims=True)
    var = jnp.square(xf - mu).mean(-1, keepdims=True)
    return ((xf - mu) * lax.rsqrt(var + LN_EPS) * g + b).astype(x.dtype)


def causal_dwconv(x, w, b):
    c = x.shape[-1]
    y = lax.conv_general_dilated(x, w[:, None, :], window_strides=(1,), padding=[(CONV_K - 1, 0)],
                                 dimension_numbers=('NWC', 'WIO', 'NWC'), feature_group_count=c)
    return y + b


def headwise(x, w):
    bsz, s = x.shape[:2]
    xr = x.reshape(bsz, s, -1, QKV_BLOCK)
    return jnp.einsum('bsnj,nij->bsni', xr, w).reshape(bsz, s, -1)


def to_chunks(t):
    bsz, s, h = t.shape[:3]
    t = t.astype(jnp.float32).reshape(bsz, s // CHUNK, CHUNK, h, *t.shape[3:])
    perm = (1, 0, 3, 2) + tuple(range(4, t.ndim))
    return t.transpose(perm)


def mlstm_chunkwise(q, k, v, i_pre, f_pre):
    bsz, s, h, dh = q.shape
    log_i = i_pre.astype(jnp.float32)
    log_f = jax.nn.log_sigmoid(f_pre.astype(jnp.float32))
    xs = (to_chunks(q), to_chunks(k) * dh ** -0.5, to_chunks(v), to_chunks(log_i), to_chunks(log_f))
    tri = jnp.tril(jnp.ones((CHUNK, CHUNK), dtype=bool))

    def step(carry, chunk):
        c_mat, n_vec, m = carry
        qc, kc, vc, li, lf = chunk
        b = jnp.cumsum(lf, axis=-1)
        g = b[..., -1]
        d_intra = jnp.where(tri, b[..., :, None] - b[..., None, :] + li[..., None, :], -jnp.inf)
        a_inter = b + m[..., None]
        m_t = jnp.maximum(a_inter, d_intra.max(-1))
        w_intra = jnp.exp(d_intra - m_t[..., None])
        w_inter = jnp.exp(a_inter - m_t)
        sc = jnp.einsum('bhld,bhsd->bhls', qc, kc) * w_intra
        num = w_inter[..., None] * jnp.einsum('bhvk,bhlk->bhlv', c_mat, qc) + jnp.einsum('bhls,bhsv->bhlv', sc, vc)
        den = w_inter * jnp.einsum('bhk,bhlk->bhl', n_vec, qc) + sc.sum(-1)
        h_out = num / jnp.maximum(jnp.abs(den), jnp.exp(-m_t))[..., None]
        log_w = g[..., None] - b + li
        m_new = jnp.maximum(g + m, log_w.max(-1))
        w_s = jnp.exp(log_w - m_new[..., None])
        decay = jnp.exp(g + m - m_new)
        c_mat = decay[..., None, None] * c_mat + jnp.einsum('bhsv,bhsk->bhvk', w_s[..., None] * vc, kc)
        n_vec = decay[..., None] * n_vec + jnp.einsum('bhs,bhsk->bhk', w_s, kc)
        return (c_mat, n_vec, m_new), h_out

    init = (jnp.zeros((bsz, h, dh, dh), jnp.float32), jnp.zeros((bsz, h, dh), jnp.float32),
            jnp.zeros((bsz, h), jnp.float32))
    _, hs = lax.scan(step, init, xs)
    return hs.transpose(1, 0, 3, 2, 4).reshape(bsz, s, h, dh)


def mlstm_layer(x, w_in, conv_w, conv_b, wq, wk, wv, w_if, b_if, gn_g, skip, w_out):
    bsz, s, _ = x.shape
    xm, z, o_pre = jnp.split(x @ w_in, 3, axis=-1)
    xc = jax.nn.silu(causal_dwconv(xm, conv_w, conv_b))
    q = headwise(xc, wq)
    k = headwise(xc, wk)
    v = headwise(xm, wv)
    gates = jnp.concatenate([q, k, v], axis=-1) @ w_if + b_if
    i_pre, f_pre = jnp.split(gates, 2, axis=-1)
    shp = lambda t: t.reshape(bsz, s, HEADS_A, HEAD_DIM_A)
    h_tilde = mlstm_chunkwise(shp(q), shp(k), shp(v), i_pre, f_pre)
    h = jax.nn.sigmoid(shp(o_pre).astype(jnp.float32)) * h_tilde
    mu = h.mean(-1, keepdims=True)
    var = jnp.square(h - mu).mean(-1, keepdims=True)
    h = ((h - mu) * lax.rsqrt(var + LN_EPS)).reshape(bsz, s, INNER_A) * gn_g
    y = (h.astype(x.dtype) + skip * xc) * jax.nn.silu(z)
    return y @ w_out


def to_residue_blocks(t, dilation):
    bsz, s = t.shape[:2]
    rest = t.shape[2:]
    u = s // dilation
    nb = -(-u // BLOCK_B)
    t = t.reshape(bsz, u, dilation, *rest)
    t = jnp.moveaxis(t, 2, 1).reshape(bsz * dilation, u, *rest)
    t = jnp.pad(t, [(0, 0), (0, nb * BLOCK_B - u)] + [(0, 0)] * len(rest))
    return t.reshape(bsz * dilation, nb, BLOCK_B, *rest)


def from_residue_blocks(t, bsz, s, dilation):
    u = s // dilation
    rest = t.shape[3:]
    t = t.reshape(bsz * dilation, -1, *rest)[:, :u]
    t = jnp.moveaxis(t.reshape(bsz, dilation, u, *rest), 1, 2)
    return t.reshape(bsz, s, *rest)


def with_previous_block(t):
    prev = jnp.pad(t[:, :-1], [(0, 0), (1, 0), (0, 0), (0, 0), (0, 0)])
    return jnp.concatenate([prev, t], axis=2)


def alibi_slopes(n_heads):
    return 2.0 ** (-8.0 * (jnp.arange(n_heads, dtype=jnp.float32) + 1.0) / n_heads)


def shared_kv(x, w_kv):
    bsz, s, _ = x.shape
    kv = (x @ w_kv).reshape(bsz, s, 2 * N_GROUPS_B, HEADS_B, HEAD_DIM_B)
    blocks = []
    for g, (window, dilation) in enumerate(GROUPS_B):
        kb = with_previous_block(to_residue_blocks(kv[:, :, 2 * g], dilation))
        vb = with_previous_block(to_residue_blocks(kv[:, :, 2 * g + 1], dilation))
        blocks.append((kb, vb))
    return blocks


def dilated_block_attention(qb, kb, vb, dilation, n_back, slopes):
    nb = qb.shape[1]
    sc = jnp.einsum('nbqhd,nbkhd->nbhqk', qb, kb, preferred_element_type=jnp.float32) * HEAD_DIM_B ** -0.5
    qi = jnp.arange(BLOCK_B)[:, None]
    kj = jnp.arange(2 * BLOCK_B)[None, :]
    j = qi + BLOCK_B - kj
    band = (j >= 0) & (j <= n_back)
    first = (jnp.arange(nb) == 0)[:, None, None] & (kj < BLOCK_B)[None]
    valid = band[None] & ~first
    bias = -slopes[:, None, None] * (j * dilation).astype(jnp.float32)[None]
    sc = jnp.where(valid[None, :, None], sc + bias[None, None], -jnp.inf)
    m = sc.max(-1)
    e = jnp.exp(sc - m[..., None])
    den = e.sum(-1)
    o = jnp.einsum('nbhqk,nbkhd->nbqhd', e, vb.astype(jnp.float32)) / jnp.moveaxis(den, 2, 3)[..., None]
    lse = jnp.moveaxis(m + jnp.log(den), 2, 3)
    return o, lse


def dilated_attention_layer(x, w_in, w_out, kv_blocks, slopes):
    bsz, s, _ = x.shape
    proj = x @ w_in
    q_all = proj[..., :N_GROUPS_B * INNER_B].reshape(bsz, s, N_GROUPS_B, HEADS_B, HEAD_DIM_B)
    z = proj[..., N_GROUPS_B * INNER_B:]
    outs, lses = [], []
    for g, (window, dilation) in enumerate(GROUPS_B):
        qb = to_residue_blocks(q_all[:, :, g], dilation)
        kb, vb = kv_blocks[g]
        o, lse = dilated_block_attention(qb, kb, vb, dilation, window // dilation, slopes)
        outs.append(from_residue_blocks(o, bsz, s, dilation))
        lses.append(from_residue_blocks(lse, bsz, s, dilation))
    wts = jax.nn.softmax(jnp.stack(lses), axis=0)
    o = jnp.sum(wts[..., None] * jnp.stack(outs), axis=0).reshape(bsz, s, INNER_B)
    y = o.astype(x.dtype) * jax.nn.silu(z)
    return y @ w_out


def setup_inputs(seed: int = 0) -> dict:
    key = jax.random.key(seed)
    ks = jax.random.split(key, 20)
    nrm = jax.random.normal
    f32 = jnp.float32
    x = nrm(ks[0], (BATCH, SEQ, D_MODEL), f32)
    ln_g = 1.0 + 0.02 * nrm(ks[1], (DEPTH, D_MODEL), f32)
    ln_b = 0.02 * nrm(ks[2], (DEPTH, D_MODEL), f32)
    a_w_in = nrm(ks[3], (N_A_LAYERS, D_MODEL, 3 * INNER_A), f32) * D_MODEL ** -0.5
    a_conv_w = nrm(ks[4], (N_A_LAYERS, CONV_K, INNER_A), f32) * CONV_K ** -0.5
    a_conv_b = 0.02 * nrm(ks[5], (N_A_LAYERS, INNER_A), f32)
    nblk = INNER_A // QKV_BLOCK
    a_wq = nrm(ks[6], (N_A_LAYERS, nblk, QKV_BLOCK, QKV_BLOCK), f32) * QKV_BLOCK ** -0.5
    a_wk = nrm(ks[7], (N_A_LAYERS, nblk, QKV_BLOCK, QKV_BLOCK), f32) * QKV_BLOCK ** -0.5
    a_wv = nrm(ks[8], (N_A_LAYERS, nblk, QKV_BLOCK, QKV_BLOCK), f32) * QKV_BLOCK ** -0.5
    a_w_if = nrm(ks[9], (N_A_LAYERS, 3 * INNER_A, 2 * HEADS_A), f32) * (3 * INNER_A) ** -0.5
    a_b_if = jnp.concatenate([0.1 * nrm(ks[10], (N_A_LAYERS, HEADS_A), f32),
                              jnp.linspace(3.0, 6.0, HEADS_A, dtype=f32)[None]
                              + 0.1 * nrm(ks[11], (N_A_LAYERS, HEADS_A), f32)], axis=-1)
    a_gn_g = 1.0 + 0.02 * nrm(ks[12], (N_A_LAYERS, INNER_A), f32)
    a_skip = 1.0 + 0.02 * nrm(ks[13], (N_A_LAYERS, INNER_A), f32)
    a_w_out = nrm(ks[14], (N_A_LAYERS, INNER_A, D_MODEL), f32) * INNER_A ** -0.5 * BETA
    b_w_kv = nrm(ks[15], (D_MODEL, 2 * N_GROUPS_B * INNER_B), f32) * D_MODEL ** -0.5
    b_w_in = nrm(ks[16], (N_B_LAYERS, D_MODEL, (N_GROUPS_B + 1) * INNER_B), f32) * D_MODEL ** -0.5
    b_w_out = nrm(ks[17], (N_B_LAYERS, INNER_B, D_MODEL), f32) * INNER_B ** -0.5 * BETA
    return {'x': x, 'ln_g': ln_g, 'ln_b': ln_b, 'a_w_in': a_w_in, 'a_conv_w': a_conv_w, 'a_conv_b': a_conv_b,
            'a_wq': a_wq, 'a_wk': a_wk, 'a_wv': a_wv, 'a_w_if': a_w_if, 'a_b_if': a_b_if, 'a_gn_g': a_gn_g,
            'a_skip': a_skip, 'a_w_out': a_w_out, 'b_w_kv': b_w_kv, 'b_w_in': b_w_in, 'b_w_out': b_w_out}


def reference(x, ln_g, ln_b, a_w_in, a_conv_w, a_conv_b, a_wq, a_wk, a_wv, a_w_if, a_b_if, a_gn_g, a_skip,
              a_w_out, b_w_kv, b_w_in, b_w_out):
    slopes = alibi_slopes(HEADS_B)
    kv_blocks = None
    for layer in range(DEPTH):
        if layer < N_A_LAYERS:
            y = mlstm_layer(x, a_w_in[layer], a_conv_w[layer], a_conv_b[layer], a_wq[layer], a_wk[layer],
                            a_wv[layer], a_w_if[layer], a_b_if[layer], a_gn_g[layer], a_skip[layer],
                            a_w_out[layer])
        else:
            lb = layer - N_A_LAYERS
            y = dilated_attention_layer(x, b_w_in[lb], b_w_out[lb], kv_blocks, slopes)
        x = layer_norm(ALPHA * x + y, ln_g[layer], ln_b[layer])
        if layer == N_A_LAYERS - 1:
            kv_blocks = shared_kv(x, b_w_kv)
    return x
```

```python
import functools

import jax
import jax.numpy as jnp
from jax import lax
from jax.experimental import pallas as pl
from jax.experimental.pallas import tpu as pltpu

D_MODEL = 1024
BATCH = 4
SEQ = 4096
DEPTH = 4
N_A_LAYERS = DEPTH // 2
N_B_LAYERS = DEPTH - N_A_LAYERS
INNER_A = 2 * D_MODEL
HEADS_A = 4
HEAD_DIM_A = INNER_A // HEADS_A
QKV_BLOCK = 4
CONV_K = 4
HEAD_DIM_B = 128
HEADS_B = D_MODEL // HEAD_DIM_B
INNER_B = HEADS_B * HEAD_DIM_B
GROUPS_B = ((128, 1), (512, 4), (2048, 16))
N_GROUPS_B = len(GROUPS_B)
BLOCK_B = 128
ALPHA = (2 * DEPTH) ** 0.25
LN_EPS = 1e-5

TOKENS = BATCH * SEQ
LANES = 128
SUBLANES = 8
CHUNK_A = 128
N_CHUNKS_A = SEQ // CHUNK_A
N_LANE_BLOCKS_A = INNER_A // LANES
VMEM_LIMIT = 48 * 1024 * 1024

F32 = jnp.float32
BF16 = jnp.bfloat16


def _dot(a, b):
    return jnp.dot(a, b, preferred_element_type=F32)


def _dot_nt(a, b):
    return lax.dot_general(a, b, (((1,), (1,)), ((), ())), preferred_element_type=F32)


def _sigmoid(x):
    return 1.0 / (1.0 + jnp.exp(-x))


def _layer_norm_rows(r, g, b):
    mu = jnp.mean(r, axis=-1, keepdims=True)
    d = r - mu
    var = jnp.mean(d * d, axis=-1, keepdims=True)
    return d * lax.rsqrt(var + LN_EPS) * g + b


def _proj_kernel(x_ref, w_ref, o_ref, xb_ref):
    @pl.when(pl.program_id(1) == 0)
    def _():
        xb_ref[...] = x_ref[...].astype(BF16)

    o_ref[...] = _dot(xb_ref[...], w_ref[...]).astype(o_ref.dtype)


def _proj(x, w, name, tm=1024, tn=1024):
    m, k = x.shape
    n = w.shape[1]
    return pl.pallas_call(
        _proj_kernel,
        grid=(m // tm, n // tn),
        in_specs=[pl.BlockSpec((tm, k), lambda i, j: (i, 0)),
                  pl.BlockSpec((k, tn), lambda i, j: (0, j))],
        out_specs=pl.BlockSpec((tm, tn), lambda i, j: (i, j)),
        out_shape=jax.ShapeDtypeStruct((m, n), BF16),
        scratch_shapes=[pltpu.VMEM((tm, k), BF16)],
        compiler_params=pltpu.CompilerParams(dimension_semantics=("parallel", "arbitrary"),
                                             vmem_limit_bytes=VMEM_LIMIT),
        name=name,
    )(x, w)


def _pre_kernel(xm_ref, halo_ref, cw_ref, cb_ref, bdqk_ref, bdv_ref, wgc_ref, wgm_ref, bg_ref,
                q_ref, k_ref, v_ref, xc_ref, g_ref, ext_ref):
    tm = xm_ref.shape[0]
    i = pl.program_id(0)
    seq_start = (i * tm) % SEQ == 0
    halo = jnp.where(seq_start, 0.0, halo_ref[...].astype(F32))
    ext_ref[0:SUBLANES, :] = halo
    ext_ref[SUBLANES:, :] = xm_ref[...].astype(F32)
    acc = cb_ref[...] + cw_ref[CONV_K - 1:CONV_K, :] * ext_ref[SUBLANES:, :]
    for j in range(CONV_K - 1):
        off = SUBLANES - (CONV_K - 1) + j
        acc = acc + cw_ref[j:j + 1, :] * ext_ref[pl.ds(off, tm), :]
    xc_ref[...] = (acc * _sigmoid(acc)).astype(BF16)

    for blk in range(N_LANE_BLOCKS_A):
        sl = slice(blk * LANES, (blk + 1) * LANES)
        qk = _dot(xc_ref[:, sl], bdqk_ref[blk])
        q_ref[:, sl] = qk[:, :LANES].astype(BF16)
        k_ref[:, sl] = qk[:, LANES:].astype(BF16)
        v_ref[:, sl] = _dot(xm_ref[:, sl], bdv_ref[blk]).astype(BF16)

    gates = _dot(xc_ref[...], wgc_ref[...]) + _dot(xm_ref[...], wgm_ref[...]) + bg_ref[...]
    for r in range(tm // LANES):
        t = gates[r * LANES:(r + 1) * LANES, :].T
        cols = slice(r * LANES, (r + 1) * LANES)
        for h in range(HEADS_A):
            g_ref[h, 0:1, cols] = t[h:h + 1, :]
            f_pre = t[HEADS_A + h:HEADS_A + h + 1, :]
            g_ref[h, 1:2, cols] = jnp.minimum(f_pre, 0.0) - jnp.log(1.0 + jnp.exp(-jnp.abs(f_pre)))


def _mlstm_pre(proj, cw, cb, bdqk, bdv, wgc, wgm, bg, name, tm=256):
    n_rows = proj.shape[0]
    act = jax.ShapeDtypeStruct((n_rows, INNER_A), BF16)
    row_spec = pl.BlockSpec((tm, INNER_A), lambda i: (i, 0))
    full = lambda shape: pl.BlockSpec(shape, lambda i: (0,) * len(shape))
    return pl.pallas_call(
        _pre_kernel,
        grid=(n_rows // tm,),
        in_specs=[row_spec,
                  pl.BlockSpec((SUBLANES, INNER_A),
                               lambda i: (jnp.maximum(i * (tm // SUBLANES) - 1, 0), 0)),
                  full((CONV_K, INNER_A)), full((1, INNER_A)),
                  full((N_LANE_BLOCKS_A, LANES, 2 * LANES)), full((N_LANE_BLOCKS_A, LANES, LANES)),
                  full((INNER_A, LANES)), full((INNER_A, LANES)), full((1, LANES))],
        out_specs=[row_spec, row_spec, row_spec, row_spec,
                   pl.BlockSpec((HEADS_A, 2, tm), lambda i: (0, 0, i))],
        out_shape=[act, act, act, act, jax.ShapeDtypeStruct((HEADS_A, 2, n_rows), F32)],
        scratch_shapes=[pltpu.VMEM((tm + SUBLANES, INNER_A), F32)],
        compiler_params=pltpu.CompilerParams(dimension_semantics=("parallel",),
                                             vmem_limit_bytes=VMEM_LIMIT),
        name=name,
    )(proj, proj, cw, cb, bdqk, bdv, wgc, wgm, bg)


def _mlstm_kernel(q_ref, k_ref, v_ref, xc_ref, z_ref, o_ref, g_ref, gn_ref, sk_ref, y_ref,
                  ct_ref, n_ref, m_ref):
    L = q_ref.shape[0]
    scale = HEAD_DIM_A ** -0.5

    @pl.when(pl.program_id(2) == 0)
    def _():
        ct_ref[...] = jnp.zeros_like(ct_ref)
        n_ref[...] = jnp.zeros_like(n_ref)
        m_ref[...] = jnp.zeros_like(m_ref)

    qb = q_ref[...]
    vb = v_ref[...]
    kf = k_ref[...].astype(F32)
    li = g_ref[0:1, :]
    lf = g_ref[1:2, :]
    m_prev = m_ref[...]

    lane = lax.broadcasted_iota(jnp.int32, (1, L), 1)
    b_row = lf
    shift = 1
    while shift < L:
        b_row = b_row + jnp.where(lane >= shift, pltpu.roll(b_row, shift, 1), 0.0)
        shift *= 2
    row = lax.broadcasted_iota(jnp.int32, (L, L), 0)
    col = lax.broadcasted_iota(jnp.int32, (L, L), 1)
    tri = col <= row
    b_col = jnp.sum(jnp.where(tri, lf, 0.0), axis=1, keepdims=True)
    li_col = jnp.sum(jnp.where(col == row, li, 0.0), axis=1, keepdims=True)
    g_tot = jnp.sum(lf, axis=1, keepdims=True)

    d_intra = jnp.where(tri, b_col - b_row + li, -jnp.inf)
    a_inter = b_col + m_prev
    m_t = jnp.maximum(a_inter, jnp.max(d_intra, axis=1, keepdims=True))
    w_intra = jnp.exp(d_intra - m_t)
    w_inter = jnp.exp(a_inter - m_t)

    sc = _dot_nt(qb, k_ref[...]) * (w_intra * scale)
    inter = _dot(qb, ct_ref[...].astype(BF16))
    num = w_inter * inter + _dot(sc.astype(BF16), vb)
    den = (w_inter * jnp.sum(qb.astype(F32) * n_ref[...], axis=1, keepdims=True)
           + jnp.sum(sc, axis=1, keepdims=True))
    h_tilde = num / jnp.maximum(jnp.abs(den), jnp.exp(-m_t))

    lw_col = g_tot - b_col + li_col
    m_new = jnp.maximum(g_tot + m_prev, jnp.max(lw_col, axis=0, keepdims=True))
    ws_col = jnp.exp(lw_col - m_new) * scale
    decay = jnp.exp(g_tot + m_prev - m_new)
    kw = kf * ws_col
    ct_ref[...] = decay * ct_ref[...] + _dot(kw.T.astype(BF16), vb)
    n_ref[...] = decay * n_ref[...] + jnp.sum(kw, axis=0, keepdims=True)
    m_ref[...] = m_new

    h = h_tilde * _sigmoid(o_ref[...].astype(F32))
    mu = jnp.mean(h, axis=1, keepdims=True)
    dlt = h - mu
    var = jnp.mean(dlt * dlt, axis=1, keepdims=True)
    hn = dlt * lax.rsqrt(var + LN_EPS) * gn_ref[...]
    zz = z_ref[...].astype(F32)
    y = (hn + sk_ref[...] * xc_ref[...].astype(F32)) * (zz * _sigmoid(zz))
    y_ref[...] = y.astype(BF16)


def _mlstm_core(q, k, v, xc, proj, gates, gn, skip, name):
    L = CHUNK_A
    dh = HEAD_DIM_A
    row = lambda b, h, c: b * N_CHUNKS_A + c
    act_spec = pl.BlockSpec((L, dh), lambda b, h, c: (row(b, h, c), h))
    vec_spec = pl.BlockSpec((1, dh), lambda b, h, c: (0, h))
    return pl.pallas_call(
        _mlstm_kernel,
        grid=(BATCH, HEADS_A, N_CHUNKS_A),
        in_specs=[act_spec, act_spec, act_spec, act_spec,
                  pl.BlockSpec((L, dh), lambda b, h, c: (row(b, h, c), HEADS_A + h)),
                  pl.BlockSpec((L, dh), lambda b, h, c: (row(b, h, c), 2 * HEADS_A + h)),
                  pl.BlockSpec((None, 2, L), lambda b, h, c: (h, 0, row(b, h, c))),
                  vec_spec, vec_spec],
        out_specs=act_spec,
        out_shape=jax.ShapeDtypeStruct((TOKENS, INNER_A), BF16),
        scratch_shapes=[pltpu.VMEM((dh, dh), F32), pltpu.VMEM((1, dh), F32), pltpu.VMEM((1, 1), F32)],
        compiler_params=pltpu.CompilerParams(
            dimension_semantics=("parallel", "parallel", "arbitrary"), vmem_limit_bytes=VMEM_LIMIT),
        name=name,
    )(q, k, v, xc, proj, proj, gates, gn, skip)


def _out_ln_kernel(y_ref, w_ref, x_ref, g_ref, b_ref, o_ref):
    r = ALPHA * x_ref[...] + _dot(y_ref[...], w_ref[...])
    o_ref[...] = _layer_norm_rows(r, g_ref[...], b_ref[...])


def _out_ln(y, w, x, g, b, name, tm=512):
    m, k = y.shape
    n = w.shape[1]
    return pl.pallas_call(
        _out_ln_kernel,
        grid=(m // tm,),
        in_specs=[pl.BlockSpec((tm, k), lambda i: (i, 0)),
                  pl.BlockSpec((k, n), lambda i: (0, 0)),
                  pl.BlockSpec((tm, n), lambda i: (i, 0)),
                  pl.BlockSpec((1, n), lambda i: (0, 0)),
                  pl.BlockSpec((1, n), lambda i: (0, 0))],
        out_specs=pl.BlockSpec((tm, n), lambda i: (i, 0)),
        out_shape=jax.ShapeDtypeStruct((m, n), F32),
        compiler_params=pltpu.CompilerParams(dimension_semantics=("parallel",),
                                             vmem_limit_bytes=VMEM_LIMIT),
        name=name,
    )(y, w, x, g, b)


def _attn_kernel(q_ref, kvp_ref, kvc_ref, o_ref, lse_ref, *, dilation):
    n = BLOCK_B
    first_block = pl.program_id(2) == 0
    qi = lax.broadcasted_iota(jnp.int32, (n, n), 0)
    kj = lax.broadcasted_iota(jnp.int32, (n, n), 1)
    dist_prev = jnp.where(kj >= qi, (qi + n - kj).astype(F32), jnp.inf)
    dist_prev = jnp.where(first_block, jnp.inf, dist_prev)
    dist_cur = jnp.where(kj <= qi, (qi - kj).astype(F32), jnp.inf)
    lse_all = jnp.zeros((n, LANES), F32)
    for h in range(HEADS_B):
        slope = 2.0 ** (-8.0 * (h + 1.0) / HEADS_B) * dilation
        ksl = slice(h * HEAD_DIM_B, (h + 1) * HEAD_DIM_B)
        vsl = slice(INNER_B + h * HEAD_DIM_B, INNER_B + (h + 1) * HEAD_DIM_B)
        qh = q_ref[:, ksl]
        s_prev = _dot_nt(qh, kvp_ref[:, ksl]) - slope * dist_prev
        s_cur = _dot_nt(qh, kvc_ref[:, ksl]) - slope * dist_cur
        m = jnp.maximum(jnp.max(s_prev, axis=1, keepdims=True), jnp.max(s_cur, axis=1, keepdims=True))
        e_prev = jnp.exp(s_prev - m)
        e_cur = jnp.exp(s_cur - m)
        den = jnp.sum(e_prev, axis=1, keepdims=True) + jnp.sum(e_cur, axis=1, keepdims=True)
        o = _dot(e_prev.astype(BF16), kvp_ref[:, vsl]) + _dot(e_cur.astype(BF16), kvc_ref[:, vsl])
        o_ref[:, ksl] = (o / den).astype(BF16)
        lse_all = jnp.where(kj == h, m + jnp.log(den), lse_all)
    lse_ref[...] = lse_all


def _attn_group(projb, kv, group, name):
    _, dilation = GROUPS_B[group]
    u = SEQ // dilation
    nb = u // BLOCK_B
    qv = projb.reshape(BATCH, u, dilation * (N_GROUPS_B + 1) * INNER_B)
    kvv = kv.reshape(BATCH, u, dilation * 2 * N_GROUPS_B * INNER_B)
    q_cols = N_GROUPS_B + 1
    o, lse = pl.pallas_call(
        functools.partial(_attn_kernel, dilation=float(dilation)),
        grid=(BATCH, dilation, nb),
        in_specs=[pl.BlockSpec((None, BLOCK_B, INNER_B), lambda b, r, ub: (b, ub, r * q_cols + group)),
                  pl.BlockSpec((None, BLOCK_B, 2 * INNER_B),
                               lambda b, r, ub: (b, jnp.maximum(ub - 1, 0), r * N_GROUPS_B + group)),
                  pl.BlockSpec((None, BLOCK_B, 2 * INNER_B),
                               lambda b, r, ub: (b, ub, r * N_GROUPS_B + group))],
        out_specs=[pl.BlockSpec((None, BLOCK_B, INNER_B), lambda b, r, ub: (b, ub, r)),
                   pl.BlockSpec((None, BLOCK_B, LANES), lambda b, r, ub: (b, ub, r))],
        out_shape=[jax.ShapeDtypeStruct((BATCH, u, dilation * INNER_B), BF16),
                   jax.ShapeDtypeStruct((BATCH, u, dilation * LANES), F32)],
        compiler_params=pltpu.CompilerParams(
            dimension_semantics=("parallel", "parallel", "arbitrary"), vmem_limit_bytes=VMEM_LIMIT),
        name=name,
    )(qv, kvv, kvv)
    return o.reshape(TOKENS, INNER_B), lse.reshape(TOKENS, LANES)


def _merge_out_ln_kernel(o0_ref, o1_ref, o2_ref, l0_ref, l1_ref, l2_ref, z_ref, w_ref, x_ref,
                         g_ref, b_ref, out_ref, y_ref):
    l0, l1, l2 = l0_ref[...], l1_ref[...], l2_ref[...]
    mx = jnp.maximum(jnp.maximum(l0, l1), l2)
    e0, e1, e2 = jnp.exp(l0 - mx), jnp.exp(l1 - mx), jnp.exp(l2 - mx)
    tot = e0 + e1 + e2
    w0, w1, w2 = e0 / tot, e1 / tot, e2 / tot
    for h in range(HEADS_B):
        sl = slice(h * HEAD_DIM_B, (h + 1) * HEAD_DIM_B)
        o = (w0[:, h:h + 1] * o0_ref[:, sl].astype(F32) + w1[:, h:h + 1] * o1_ref[:, sl].astype(F32)
             + w2[:, h:h + 1] * o2_ref[:, sl].astype(F32))
        zz = z_ref[:, sl].astype(F32)
        y_ref[:, sl] = (o * (zz * _sigmoid(zz))).astype(BF16)
    r = ALPHA * x_ref[...] + _dot(y_ref[...], w_ref[...])
    out_ref[...] = _layer_norm_rows(r, g_ref[...], b_ref[...])


def _merge_out_ln(os_, lses, projb, w, x, g, b, name, tm=256):
    m = x.shape[0]
    n = D_MODEL
    o_spec = pl.BlockSpec((tm, INNER_B), lambda i: (i, 0))
    l_spec = pl.BlockSpec((tm, LANES), lambda i: (i, 0))
    vec = pl.BlockSpec((1, n), lambda i: (0, 0))
    return pl.pallas_call(
        _merge_out_ln_kernel,
        grid=(m // tm,),
        in_specs=[o_spec, o_spec, o_spec, l_spec, l_spec, l_spec,
                  pl.BlockSpec((tm, INNER_B), lambda i: (i, N_GROUPS_B)),
                  pl.BlockSpec((INNER_B, n), lambda i: (0, 0)),
                  pl.BlockSpec((tm, n), lambda i: (i, 0)), vec, vec],
        out_specs=pl.BlockSpec((tm, n), lambda i: (i, 0)),
        out_shape=jax.ShapeDtypeStruct((m, n), F32),
        scratch_shapes=[pltpu.VMEM((tm, INNER_B), BF16)],
        compiler_params=pltpu.CompilerParams(dimension_semantics=("parallel",),
                                             vmem_limit_bytes=VMEM_LIMIT),
        name=name,
    )(*os_, *lses, projb, w, x, g, b)


def _block_diag_lane_blocks(w):
    per = LANES // QKV_BLOCK
    w4 = w.reshape(-1, per, QKV_BLOCK, QKV_BLOCK)
    eye = jnp.eye(per, dtype=w.dtype)
    dense = jnp.einsum('bmij,mp->bmjpi', w4, eye)
    return dense.reshape(-1, LANES, LANES)


def _fold_gate_weights(wq, wk, wv, w_if):
    nblk = INNER_A // QKV_BLOCK
    wif = w_if.reshape(3, nblk, QKV_BLOCK, 2 * HEADS_A)
    hp = lax.Precision.HIGHEST
    wgc = (jnp.einsum('nij,nio->njo', wq, wif[0], precision=hp)
           + jnp.einsum('nij,nio->njo', wk, wif[1], precision=hp)).reshape(INNER_A, 2 * HEADS_A)
    wgm = jnp.einsum('nij,nio->njo', wv, wif[2], precision=hp).reshape(INNER_A, 2 * HEADS_A)
    pad = ((0, 0), (0, LANES - 2 * HEADS_A))
    return jnp.pad(wgc, pad).astype(BF16), jnp.pad(wgm, pad).astype(BF16)


def kernel(x, ln_g, ln_b, a_w_in, a_conv_w, a_conv_b, a_wq, a_wk, a_wv, a_w_if, a_b_if, a_gn_g, a_skip,
           a_w_out, b_w_kv, b_w_in, b_w_out):
    xs = x.reshape(TOKENS, D_MODEL)
    for layer in range(N_A_LAYERS):
        w_in = a_w_in[layer].astype(BF16)
        bdqk = jnp.concatenate([_block_diag_lane_blocks(a_wq[layer]),
                                _block_diag_lane_blocks(a_wk[layer])], axis=-1).astype(BF16)
        bdv = _block_diag_lane_blocks(a_wv[layer]).astype(BF16)
        wgc, wgm = _fold_gate_weights(a_wq[layer], a_wk[layer], a_wv[layer], a_w_if[layer])
        bg = jnp.pad(a_b_if[layer][None, :], ((0, 0), (0, LANES - 2 * HEADS_A)))
        proj = _proj(xs, w_in, f"a{layer}_proj")
        q, k, v, xc, gates = _mlstm_pre(proj, a_conv_w[layer], a_conv_b[layer][None, :], bdqk, bdv,
                                        wgc, wgm, bg, f"a{layer}_pre")
        y = _mlstm_core(q, k, v, xc, proj, gates, a_gn_g[layer][None, :], a_skip[layer][None, :],
                        f"a{layer}_mlstm")
        xs = _out_ln(y, a_w_out[layer].astype(BF16), xs, ln_g[layer][None, :], ln_b[layer][None, :],
                     f"a{layer}_out_ln")

    kv = _proj(xs, b_w_kv.astype(BF16), "kv_proj")
    q_scale = jnp.concatenate([jnp.full((N_GROUPS_B * INNER_B,), HEAD_DIM_B ** -0.5, F32),
                               jnp.ones((INNER_B,), F32)])
    for lb in range(N_B_LAYERS):
        layer = N_A_LAYERS + lb
        w_in = (b_w_in[lb] * q_scale[None, :]).astype(BF16)
        projb = _proj(xs, w_in, f"b{lb}_proj")
        outs, lses = [], []
        for g in range(N_GROUPS_B):
            o, lse = _attn_group(projb, kv, g, f"b{lb}_attn{g}")
            outs.append(o)
            lses.append(lse)
        xs = _merge_out_ln(outs, lses, projb, b_w_out[lb].astype(BF16), xs, ln_g[layer][None, :],
                           ln_b[layer][None, :], f"b{lb}_merge_out_ln")
    return xs.reshape(BATCH, SEQ, D_MODEL)
```

```python
import functools

import jax
import jax.numpy as jnp
from jax import lax
from jax.experimental import pallas as pl
from jax.experimental.pallas import tpu as pltpu

D_MODEL = 1024
BATCH = 4
SEQ = 4096
DEPTH = 4
N_A_LAYERS = DEPTH // 2
N_B_LAYERS = DEPTH - N_A_LAYERS
INNER_A = 2 * D_MODEL
HEADS_A = 4
HEAD_DIM_A = INNER_A // HEADS_A
QKV_BLOCK = 4
CONV_K = 4
HEAD_DIM_B = 128
HEADS_B = D_MODEL // HEAD_DIM_B
INNER_B = HEADS_B * HEAD_DIM_B
GROUPS_B = ((128, 1), (512, 4), (2048, 16))
N_GROUPS_B = len(GROUPS_B)
DILATIONS = tuple(d for _, d in GROUPS_B)
BLOCK_B = 128
ALPHA = (2 * DEPTH) ** 0.25
LN_EPS = 1e-5

TOKENS = BATCH * SEQ
LANES = 128
SUBLANES = 8
N_SLABS = D_MODEL // LANES
CHUNK_A = 128
N_CHUNKS_A = SEQ // CHUNK_A
N_LANE_BLOCKS_A = INNER_A // LANES
VMEM_LIMIT = 48 * 1024 * 1024
CV_W_INTER, CV_FLOOR, CV_W_STATE, CV_DECAY = 0, HEADS_A, 2 * HEADS_A, 3 * HEADS_A

F32 = jnp.float32
BF16 = jnp.bfloat16

assert all(w // d == BLOCK_B for w, d in GROUPS_B)


def _dot(a, b):
    return jnp.dot(a, b, preferred_element_type=F32)


def _dot_nt(a, b):
    return lax.dot_general(a, b, (((1,), (1,)), ((), ())), preferred_element_type=F32)


def _sigmoid(x):
    return 1.0 / (1.0 + jnp.exp(-x))


def _layer_norm_rows(r, g, b):
    mu = jnp.mean(r, axis=-1, keepdims=True)
    d = r - mu
    var = jnp.mean(d * d, axis=-1, keepdims=True)
    return d * lax.rsqrt(var + LN_EPS) * g + b


def _store_stream(r, x_ref, perm_refs, slab_ref):
    x_ref[...] = r
    if not perm_refs:
        return
    perm_refs[0][...] = r.astype(BF16)
    tm = r.shape[0]
    for s in range(N_SLABS):
        slab_ref[s] = r[:, s * LANES:(s + 1) * LANES]
    for ref, d in zip(perm_refs[1:], DILATIONS[1:]):
        for res in range(d):
            for s in range(N_SLABS):
                ref[res, :, s * LANES:(s + 1) * LANES] = (
                    slab_ref[s, pl.ds(res, tm // d, stride=d), :].astype(BF16))


def _stream_out(tm, with_perm):
    per_seq = SEQ // tm
    shapes = [jax.ShapeDtypeStruct((TOKENS, D_MODEL), F32)]
    specs = [pl.BlockSpec((tm, D_MODEL), lambda i: (i, 0))]
    if with_perm:
        shapes.append(jax.ShapeDtypeStruct((TOKENS, D_MODEL), BF16))
        specs.append(pl.BlockSpec((tm, D_MODEL), lambda i: (i, 0)))
        for d in DILATIONS[1:]:
            shapes.append(jax.ShapeDtypeStruct((BATCH, d, SEQ // d, D_MODEL), BF16))
            specs.append(pl.BlockSpec((None, d, tm // d, D_MODEL),
                                      lambda i: (i // per_seq, 0, i % per_seq, 0)))
    return shapes, specs


def _proj_kernel(x_ref, w_ref, o_ref):
    o_ref[...] = _dot(x_ref[...].astype(BF16), w_ref[...]).astype(o_ref.dtype)


def _proj(x, w, name, tm=1024, tn=1024):
    m, k = x.shape
    n = w.shape[1]
    tn = min(tn, n)
    return pl.pallas_call(
        _proj_kernel,
        grid=(m // tm, n // tn),
        in_specs=[pl.BlockSpec((tm, k), lambda i, j: (i, 0)),
                  pl.BlockSpec((k, tn), lambda i, j: (0, j))],
        out_specs=pl.BlockSpec((tm, tn), lambda i, j: (i, j)),
        out_shape=jax.ShapeDtypeStruct((m, n), BF16),
        compiler_params=pltpu.CompilerParams(dimension_semantics=("parallel", "arbitrary"),
                                             vmem_limit_bytes=VMEM_LIMIT),
        name=name,
    )(x, w)


def _pre_kernel(xm_ref, halo_ref, cw_ref, cb_ref, bdqk_ref, bdv_ref, wgc_ref, wgm_ref, bg_ref,
                q_ref, k_ref, v_ref, xc_ref, wi_ref, cv_ref, ext_ref, m_ref):
    tm = xm_ref.shape[0]
    L = CHUNK_A
    scale = HEAD_DIM_A ** -0.5
    i = pl.program_id(0)
    seq_start = (i * tm) % SEQ == 0

    @pl.when(seq_start)
    def _():
        m_ref[...] = jnp.zeros_like(m_ref)

    halo = jnp.where(seq_start, 0.0, halo_ref[...].astype(F32))
    ext_ref[0:SUBLANES, :] = halo
    ext_ref[SUBLANES:, :] = xm_ref[...].astype(F32)
    acc = cb_ref[...] + cw_ref[CONV_K - 1:CONV_K, :] * ext_ref[SUBLANES:, :]
    for j in range(CONV_K - 1):
        off = SUBLANES - (CONV_K - 1) + j
        acc = acc + cw_ref[j:j + 1, :] * ext_ref[pl.ds(off, tm), :]
    xc_ref[...] = (acc * _sigmoid(acc)).astype(BF16)

    for blk in range(N_LANE_BLOCKS_A):
        sl = slice(blk * LANES, (blk + 1) * LANES)
        qk = _dot(xc_ref[:, sl], bdqk_ref[blk])
        q_ref[:, sl] = qk[:, :LANES].astype(BF16)
        k_ref[:, sl] = qk[:, LANES:].astype(BF16)
        v_ref[:, sl] = _dot(xm_ref[:, sl], bdv_ref[blk]).astype(BF16)

    gates = _dot(xc_ref[...], wgc_ref[...]) + _dot(xm_ref[...], wgm_ref[...]) + bg_ref[...]
    log_sig = jnp.minimum(gates, 0.0) - jnp.log(1.0 + jnp.exp(-jnp.abs(gates)))
    row = lax.broadcasted_iota(jnp.int32, (L, L), 0)
    col = lax.broadcasted_iota(jnp.int32, (L, L), 1)
    tri = col <= row
    lane = lax.broadcasted_iota(jnp.int32, (SUBLANES, L), 1)
    cv_ref[...] = jnp.zeros_like(cv_ref)
    for c in range(tm // L):
        rows = slice(c * L, (c + 1) * L)
        g_cols = gates[rows, :]
        li_rows = g_cols.T[0:SUBLANES, :]
        lf_rows = log_sig[rows, :].T[0:SUBLANES, :]
        b_rows = lf_rows
        shift = 1
        while shift < L:
            b_rows = b_rows + jnp.where(lane >= shift, pltpu.roll(b_rows, shift, 1), 0.0)
            shift *= 2
        for h in range(HEADS_A):
            li_row = li_rows[h:h + 1, :]
            lf_row = lf_rows[HEADS_A + h:HEADS_A + h + 1, :]
            b_row = b_rows[HEADS_A + h:HEADS_A + h + 1, :]
            li_col = g_cols[:, h:h + 1]
            b_col = jnp.sum(jnp.where(tri, lf_row, 0.0), axis=1, keepdims=True)
            g_tot = jnp.sum(lf_row, axis=1, keepdims=True)
            m_prev = m_ref[h:h + 1, 0:1]
            d_intra = jnp.where(tri, b_col - b_row + li_row, -jnp.inf)
            a_inter = b_col + m_prev
            m_t = jnp.maximum(a_inter, jnp.max(d_intra, axis=1, keepdims=True))
            wi_ref[h, rows, :] = jnp.exp(d_intra - m_t) * scale
            lw_col = g_tot - b_col + li_col
            m_new = jnp.maximum(g_tot + m_prev, jnp.max(lw_col, axis=0, keepdims=True))
            cv_ref[rows, CV_W_INTER + h:CV_W_INTER + h + 1] = jnp.exp(a_inter - m_t)
            cv_ref[rows, CV_FLOOR + h:CV_FLOOR + h + 1] = jnp.exp(-m_t)
            cv_ref[rows, CV_W_STATE + h:CV_W_STATE + h + 1] = jnp.exp(lw_col - m_new) * scale
            cv_ref[rows, CV_DECAY + h:CV_DECAY + h + 1] = jnp.broadcast_to(
                jnp.exp(g_tot + m_prev - m_new), (L, 1))
            m_ref[h:h + 1, 0:1] = m_new


def _mlstm_pre(proj, cw, cb, bdqk, bdv, wgc, wgm, bg, name, tm=256):
    n_rows = proj.shape[0]
    act = jax.ShapeDtypeStruct((n_rows, INNER_A), BF16)
    row_spec = pl.BlockSpec((tm, INNER_A), lambda i: (i, 0))
    full = lambda shape: pl.BlockSpec(shape, lambda i: (0,) * len(shape))
    return pl.pallas_call(
        _pre_kernel,
        grid=(n_rows // tm,),
        in_specs=[row_spec,
                  pl.BlockSpec((SUBLANES, INNER_A),
                               lambda i: (jnp.maximum(i * (tm // SUBLANES) - 1, 0), 0)),
                  full((CONV_K, INNER_A)), full((1, INNER_A)),
                  full((N_LANE_BLOCKS_A, LANES, 2 * LANES)), full((N_LANE_BLOCKS_A, LANES, LANES)),
                  full((INNER_A, LANES)), full((INNER_A, LANES)), full((1, LANES))],
        out_specs=[row_spec, row_spec, row_spec, row_spec,
                   pl.BlockSpec((HEADS_A, tm, CHUNK_A), lambda i: (0, i, 0)),
                   pl.BlockSpec((tm, LANES), lambda i: (i, 0))],
        out_shape=[act, act, act, act,
                   jax.ShapeDtypeStruct((HEADS_A, n_rows, CHUNK_A), F32),
                   jax.ShapeDtypeStruct((n_rows, LANES), F32)],
        scratch_shapes=[pltpu.VMEM((tm + SUBLANES, INNER_A), F32), pltpu.VMEM((SUBLANES, LANES), F32)],
        compiler_params=pltpu.CompilerParams(dimension_semantics=("arbitrary",),
                                             vmem_limit_bytes=VMEM_LIMIT),
        name=name,
    )(proj, proj, cw, cb, bdqk, bdv, wgc, wgm, bg)


def _mlstm_kernel(q_ref, k_ref, v_ref, xc_ref, z_ref, o_ref, wi_ref, cv_ref, gn_ref, sk_ref, y_ref,
                  ct_ref, n_ref):
    @pl.when(pl.program_id(1) == 0)
    def _():
        ct_ref[...] = jnp.zeros_like(ct_ref)
        n_ref[...] = jnp.zeros_like(n_ref)

    for h in range(HEADS_A):
        hs = slice(h * HEAD_DIM_A, (h + 1) * HEAD_DIM_A)
        qb = q_ref[:, hs]
        kb = k_ref[:, hs]
        vb = v_ref[:, hs]
        w_inter = cv_ref[:, CV_W_INTER + h:CV_W_INTER + h + 1]
        floor = cv_ref[:, CV_FLOOR + h:CV_FLOOR + h + 1]
        w_state = cv_ref[:, CV_W_STATE + h:CV_W_STATE + h + 1]
        decay = cv_ref[0:1, CV_DECAY + h:CV_DECAY + h + 1]

        sc = _dot_nt(qb, kb) * wi_ref[h]
        ct = ct_ref[h]
        num = w_inter * _dot(qb, ct.astype(BF16)) + _dot(sc.astype(BF16), vb)
        den = (w_inter * jnp.sum(qb.astype(F32) * n_ref[h], axis=1, keepdims=True)
               + jnp.sum(sc, axis=1, keepdims=True))
        h_tilde = num * (1.0 / jnp.maximum(jnp.abs(den), floor))

        kw = kb.astype(F32) * w_state
        ct_ref[h] = decay * ct + _dot(kw.T.astype(BF16), vb)
        n_ref[h] = decay * n_ref[h] + jnp.sum(kw, axis=0, keepdims=True)

        hg = h_tilde * _sigmoid(o_ref[:, hs].astype(F32))
        mu = jnp.mean(hg, axis=1, keepdims=True)
        dlt = hg - mu
        var = jnp.mean(dlt * dlt, axis=1, keepdims=True)
        hn = dlt * lax.rsqrt(var + LN_EPS) * gn_ref[:, hs]
        zz = z_ref[:, hs].astype(F32)
        y = (hn + sk_ref[:, hs] * xc_ref[:, hs].astype(F32)) * (zz * _sigmoid(zz))
        y_ref[:, hs] = y.astype(BF16)


def _mlstm_core(q, k, v, xc, proj, wi, cv, gn, skip, name):
    L = CHUNK_A
    dh = HEAD_DIM_A
    row = lambda b, c: b * N_CHUNKS_A + c
    act_spec = pl.BlockSpec((L, INNER_A), lambda b, c: (row(b, c), 0))
    vec_spec = pl.BlockSpec((1, INNER_A), lambda b, c: (0, 0))
    return pl.pallas_call(
        _mlstm_kernel,
        grid=(BATCH, N_CHUNKS_A),
        in_specs=[act_spec, act_spec, act_spec, act_spec,
                  pl.BlockSpec((L, INNER_A), lambda b, c: (row(b, c), 1)),
                  pl.BlockSpec((L, INNER_A), lambda b, c: (row(b, c), 2)),
                  pl.BlockSpec((HEADS_A, L, L), lambda b, c: (0, row(b, c), 0)),
                  pl.BlockSpec((L, LANES), lambda b, c: (row(b, c), 0)),
                  vec_spec, vec_spec],
        out_specs=act_spec,
        out_shape=jax.ShapeDtypeStruct((TOKENS, INNER_A), BF16),
        scratch_shapes=[pltpu.VMEM((HEADS_A, dh, dh), F32), pltpu.VMEM((HEADS_A, 1, dh), F32)],
        compiler_params=pltpu.CompilerParams(
            dimension_semantics=("parallel", "arbitrary"), vmem_limit_bytes=VMEM_LIMIT),
        name=name,
    )(q, k, v, xc, proj, proj, wi, cv, gn, skip)


def _out_ln_kernel(y_ref, w_ref, x_ref, g_ref, b_ref, *rest, with_perm):
    n_out = 1 + (N_GROUPS_B if with_perm else 0)
    outs, scratch = rest[:n_out], rest[n_out:]
    r = ALPHA * x_ref[...] + _dot(y_ref[...], w_ref[...])
    _store_stream(_layer_norm_rows(r, g_ref[...], b_ref[...]), outs[0], outs[1:],
                  scratch[0] if with_perm else None)


def _out_ln(y, w, x, g, b, name, with_perm, tm=512):
    m, k = y.shape
    n = w.shape[1]
    out_shape, out_specs = _stream_out(tm, with_perm)
    return pl.pallas_call(
        functools.partial(_out_ln_kernel, with_perm=with_perm),
        grid=(m // tm,),
        in_specs=[pl.BlockSpec((tm, k), lambda i: (i, 0)),
                  pl.BlockSpec((k, n), lambda i: (0, 0)),
                  pl.BlockSpec((tm, n), lambda i: (i, 0)),
                  pl.BlockSpec((1, n), lambda i: (0, 0)),
                  pl.BlockSpec((1, n), lambda i: (0, 0))],
        out_specs=out_specs,
        out_shape=out_shape,
        scratch_shapes=[pltpu.VMEM((N_SLABS, tm, LANES), F32)] if with_perm else [],
        compiler_params=pltpu.CompilerParams(dimension_semantics=("parallel",),
                                             vmem_limit_bytes=VMEM_LIMIT),
        name=name,
    )(y, w, x, g, b)


def _attn_kernel(q_ref, kvp_ref, kvc_ref, o_ref, lse_ref, *, dilation, blocks_per_seq):
    n = BLOCK_B
    first = pl.program_id(0) % blocks_per_seq == 0
    qi = lax.broadcasted_iota(jnp.int32, (n, n), 0)
    kj = lax.broadcasted_iota(jnp.int32, (n, n), 1)
    lower = kj <= qi
    diag = kj == qi
    dist = jnp.bitwise_and(qi - kj, n - 1).astype(F32)
    head = lax.broadcasted_iota(jnp.int32, (HEADS_B, 1, 1), 0).astype(F32)
    slope = jnp.exp2(-8.0 * (head + 1.0) / HEADS_B) * dilation

    ksl = [slice(h * HEAD_DIM_B, (h + 1) * HEAD_DIM_B) for h in range(HEADS_B)]
    vsl = [slice(INNER_B + h * HEAD_DIM_B, INNER_B + (h + 1) * HEAD_DIM_B) for h in range(HEADS_B)]
    s_cur = jnp.stack([_dot_nt(q_ref[:, ksl[h]], kvc_ref[:, ksl[h]]) for h in range(HEADS_B)])
    s_prev = jnp.stack([_dot_nt(q_ref[:, ksl[h]], kvp_ref[:, ksl[h]]) for h in range(HEADS_B)])
    s_prev = jnp.where(first, -jnp.inf, s_prev)
    s_diag = jnp.max(jnp.where(diag, s_prev, -jnp.inf), axis=-1, keepdims=True) - slope * float(n)
    s = jnp.where(lower, s_cur, s_prev) - slope * dist
    m = jnp.maximum(jnp.max(s, axis=-1, keepdims=True), s_diag)
    e = jnp.exp(s - m)
    e_diag = jnp.exp(s_diag - m)
    inv_den = 1.0 / (jnp.sum(e, axis=-1, keepdims=True) + e_diag)
    lse = m - jnp.log(inv_den)
    p_cur = jnp.where(lower, e, 0.0).astype(BF16)
    p_prev = jnp.where(lower, 0.0, e).astype(BF16)
    lse_all = jnp.zeros((n, LANES), F32)
    for h in range(HEADS_B):
        vp = kvp_ref[:, vsl[h]]
        o = _dot(p_cur[h], kvc_ref[:, vsl[h]]) + _dot(p_prev[h], vp) + e_diag[h] * vp.astype(F32)
        o_ref[:, ksl[h]] = (o * inv_den[h]).astype(BF16)
        lse_all = jnp.where(kj == h, lse[h], lse_all)
    lse_ref[...] = lse_all


def _attn_group(q, kv, group, name):
    dilation = DILATIONS[group]
    blocks_per_seq = SEQ // dilation // BLOCK_B
    return pl.pallas_call(
        functools.partial(_attn_kernel, dilation=float(dilation), blocks_per_seq=blocks_per_seq),
        grid=(TOKENS // BLOCK_B,),
        in_specs=[pl.BlockSpec((BLOCK_B, INNER_B), lambda n: (n, 0)),
                  pl.BlockSpec((BLOCK_B, 2 * INNER_B), lambda n: (jnp.maximum(n - 1, 0), 0)),
                  pl.BlockSpec((BLOCK_B, 2 * INNER_B), lambda n: (n, 0))],
        out_specs=[pl.BlockSpec((BLOCK_B, INNER_B), lambda n: (n, 0)),
                   pl.BlockSpec((BLOCK_B, LANES), lambda n: (n, 0))],
        out_shape=[jax.ShapeDtypeStruct((TOKENS, INNER_B), BF16),
                   jax.ShapeDtypeStruct((TOKENS, LANES), F32)],
        compiler_params=pltpu.CompilerParams(dimension_semantics=("arbitrary",),
                                             vmem_limit_bytes=VMEM_LIMIT),
        name=name,
    )(q, kv, kv)


def _merge_out_ln_kernel(o0_ref, o1_ref, o2_ref, l0_ref, l1_ref, l2_ref, z_ref, w_ref, x_ref,
                         g_ref, b_ref, *rest, with_perm):
    n_out = 1 + (N_GROUPS_B if with_perm else 0)
    outs = rest[:n_out]
    y_ref, os1_ref, os2_ref, ls1_ref, ls2_ref = rest[n_out:]
    tm = x_ref.shape[0]
    for o_ref, l_ref, os_ref, ls_ref, d in ((o1_ref, l1_ref, os1_ref, ls1_ref, DILATIONS[1]),
                                            (o2_ref, l2_ref, os2_ref, ls2_ref, DILATIONS[2])):
        for res in range(d):
            ls_ref[pl.ds(res, tm // d, stride=d), :] = l_ref[res]
            for s in range(N_SLABS):
                os_ref[s, pl.ds(res, tm // d, stride=d), :] = (
                    o_ref[res, :, s * LANES:(s + 1) * LANES].astype(F32))
    l0, l1, l2 = l0_ref[...], ls1_ref[...], ls2_ref[...]
    mx = jnp.maximum(jnp.maximum(l0, l1), l2)
    e0, e1, e2 = jnp.exp(l0 - mx), jnp.exp(l1 - mx), jnp.exp(l2 - mx)
    inv = 1.0 / (e0 + e1 + e2)
    w0, w1, w2 = e0 * inv, e1 * inv, e2 * inv
    for h in range(HEADS_B):
        sl = slice(h * HEAD_DIM_B, (h + 1) * HEAD_DIM_B)
        o = (w0[:, h:h + 1] * o0_ref[:, sl].astype(F32) + w1[:, h:h + 1] * os1_ref[h]
             + w2[:, h:h + 1] * os2_ref[h])
        zz = z_ref[:, sl].astype(F32)
        y_ref[:, sl] = (o * (zz * _sigmoid(zz))).astype(BF16)
    r = ALPHA * x_ref[...] + _dot(y_ref[...], w_ref[...])
    _store_stream(_layer_norm_rows(r, g_ref[...], b_ref[...]), outs[0], outs[1:], os1_ref)


def _merge_out_ln(os_, lses, qz, w, x, g, b, name, with_perm, tm=256):
    m = x.shape[0]
    n = D_MODEL
    per_seq = SEQ // tm
    tok = lambda width: pl.BlockSpec((tm, width), lambda i: (i, 0))
    res = lambda d, width: pl.BlockSpec((None, d, tm // d, width),
                                        lambda i: (i // per_seq, 0, i % per_seq, 0))
    vec = pl.BlockSpec((1, n), lambda i: (0, 0))
    d1, d2 = DILATIONS[1], DILATIONS[2]
    out_shape, out_specs = _stream_out(tm, with_perm)
    return pl.pallas_call(
        functools.partial(_merge_out_ln_kernel, with_perm=with_perm),
        grid=(m // tm,),
        in_specs=[tok(INNER_B), res(d1, INNER_B), res(d2, INNER_B),
                  tok(LANES), res(d1, LANES), res(d2, LANES),
                  pl.BlockSpec((tm, INNER_B), lambda i: (i, 1)),
                  pl.BlockSpec((INNER_B, n), lambda i: (0, 0)),
                  tok(n), vec, vec],
        out_specs=out_specs,
        out_shape=out_shape,
        scratch_shapes=[pltpu.VMEM((tm, INNER_B), BF16),
                        pltpu.VMEM((N_SLABS, tm, LANES), F32), pltpu.VMEM((N_SLABS, tm, LANES), F32),
                        pltpu.VMEM((tm, LANES), F32), pltpu.VMEM((tm, LANES), F32)],
        compiler_params=pltpu.CompilerParams(dimension_semantics=("parallel",),
                                             vmem_limit_bytes=VMEM_LIMIT),
        name=name,
    )(os_[0], os_[1].reshape(BATCH, d1, SEQ // d1, INNER_B), os_[2].reshape(BATCH, d2, SEQ // d2, INNER_B),
      lses[0], lses[1].reshape(BATCH, d1, SEQ // d1, LANES), lses[2].reshape(BATCH, d2, SEQ // d2, LANES),
      qz, w, x, g, b)


def _block_diag_lane_blocks(w):
    per = LANES // QKV_BLOCK
    w4 = w.reshape(-1, per, QKV_BLOCK, QKV_BLOCK)
    eye = jnp.eye(per, dtype=w.dtype)
    dense = jnp.einsum('bmij,mp->bmjpi', w4, eye)
    return dense.reshape(-1, LANES, LANES)


def _fold_gate_weights(wq, wk, wv, w_if):
    nblk = INNER_A // QKV_BLOCK
    wif = w_if.reshape(3, nblk, QKV_BLOCK, 2 * HEADS_A)
    hp = lax.Precision.HIGHEST
    wgc = (jnp.einsum('nij,nio->njo', wq, wif[0], precision=hp)
           + jnp.einsum('nij,nio->njo', wk, wif[1], precision=hp)).reshape(INNER_A, 2 * HEADS_A)
    wgm = jnp.einsum('nij,nio->njo', wv, wif[2], precision=hp).reshape(INNER_A, 2 * HEADS_A)
    pad = ((0, 0), (0, LANES - 2 * HEADS_A))
    return jnp.pad(wgc, pad).astype(BF16), jnp.pad(wgm, pad).astype(BF16)


def kernel(x, ln_g, ln_b, a_w_in, a_conv_w, a_conv_b, a_wq, a_wk, a_wv, a_w_if, a_b_if, a_gn_g, a_skip,
           a_w_out, b_w_kv, b_w_in, b_w_out):
    xs = x.reshape(TOKENS, D_MODEL)
    x_in = xs
    for layer in range(N_A_LAYERS):
        w_in = a_w_in[layer].astype(BF16)
        bdqk = jnp.concatenate([_block_diag_lane_blocks(a_wq[layer]),
                                _block_diag_lane_blocks(a_wk[layer])], axis=-1).astype(BF16)
        bdv = _block_diag_lane_blocks(a_wv[layer]).astype(BF16)
        wgc, wgm = _fold_gate_weights(a_wq[layer], a_wk[layer], a_wv[layer], a_w_if[layer])
        bg = jnp.pad(a_b_if[layer][None, :], ((0, 0), (0, LANES - 2 * HEADS_A)))
        proj = _proj(x_in, w_in, f"a{layer}_proj")
        q, k, v, xc, wi, cv = _mlstm_pre(proj, a_conv_w[layer], a_conv_b[layer][None, :], bdqk, bdv,
                                         wgc, wgm, bg, f"a{layer}_pre")
        y = _mlstm_core(q, k, v, xc, proj, wi, cv, a_gn_g[layer][None, :], a_skip[layer][None, :],
                        f"a{layer}_mlstm")
        outs = _out_ln(y, a_w_out[layer].astype(BF16), xs, ln_g[layer][None, :], ln_b[layer][None, :],
                       f"a{layer}_out_ln", with_perm=layer == N_A_LAYERS - 1)
        xs = outs[0]
        x_in = xs

    perms = [p.reshape(TOKENS, D_MODEL) for p in outs[1:]]
    kvs = [_proj(perms[g], b_w_kv[:, 2 * g * INNER_B:(2 * g + 2) * INNER_B].astype(BF16), f"kv_proj{g}",
                 tn=2 * INNER_B) for g in range(N_GROUPS_B)]
    q_scale = HEAD_DIM_B ** -0.5
    for lb in range(N_B_LAYERS):
        layer = N_A_LAYERS + lb
        w_in = b_w_in[lb]
        wq = [(w_in[:, g * INNER_B:(g + 1) * INNER_B] * q_scale).astype(BF16) for g in range(N_GROUPS_B)]
        wz = w_in[:, N_GROUPS_B * INNER_B:].astype(BF16)
        qz = _proj(perms[0], jnp.concatenate([wq[0], wz], axis=1), f"b{lb}_proj0", tn=2 * INNER_B)
        qs = [qz] + [_proj(perms[g], wq[g], f"b{lb}_proj{g}") for g in range(1, N_GROUPS_B)]
        os_, lses = [], []
        for g in range(N_GROUPS_B):
            o, lse = _attn_group(qs[g], kvs[g], g, f"b{lb}_attn{g}")
            os_.append(o)
            lses.append(lse)
        outs = _merge_out_ln(os_, lses, qz, b_w_out[lb].astype(BF16), xs, ln_g[layer][None, :],
                             ln_b[layer][None, :], f"b{lb}_merge_out_ln",
                             with_perm=lb < N_B_LAYERS - 1)
        xs = outs[0]
        perms = [p.reshape(TOKENS, D_MODEL) for p in outs[1:]]
    return xs.reshape(BATCH, SEQ, D_MODEL)
```

```python
import functools

import jax
import jax.numpy as jnp
from jax import lax
from jax.experimental import pallas as pl
from jax.experimental.pallas import tpu as pltpu

D_MODEL = 1024
BATCH = 4
SEQ = 4096
DEPTH = 4
N_A_LAYERS = DEPTH // 2
N_B_LAYERS = DEPTH - N_A_LAYERS
INNER_A = 2 * D_MODEL
HEADS_A = 4
HEAD_DIM_A = INNER_A // HEADS_A
QKV_BLOCK = 4
CONV_K = 4
HEAD_DIM_B = 128
HEADS_B = D_MODEL // HEAD_DIM_B
INNER_B = HEADS_B * HEAD_DIM_B
GROUPS_B = ((128, 1), (512, 4), (2048, 16))
N_GROUPS_B = len(GROUPS_B)
DILATIONS = tuple(d for _, d in GROUPS_B)
BLOCK_B = 128
ALPHA = (2 * DEPTH) ** 0.25
LN_EPS = 1e-5

TOKENS = BATCH * SEQ
LANES = 128
SUBLANES = 8
N_SLABS = D_MODEL // LANES
CHUNK_A = 128
N_CHUNKS_A = SEQ // CHUNK_A
N_LANE_BLOCKS_A = INNER_A // LANES
VMEM_LIMIT = 48 * 1024 * 1024
CV_W_INTER, CV_FLOOR, CV_W_STATE, CV_DECAY = 0, HEADS_A, 2 * HEADS_A, 3 * HEADS_A

F32 = jnp.float32
BF16 = jnp.bfloat16

assert all(w // d == BLOCK_B for w, d in GROUPS_B)


def _dot(a, b):
    return jnp.dot(a, b, preferred_element_type=F32)


def _dot_nt(a, b):
    return lax.dot_general(a, b, (((1,), (1,)), ((), ())), preferred_element_type=F32)


def _sigmoid(x):
    return 1.0 / (1.0 + jnp.exp(-x))


def _layer_norm_rows(r, g, b):
    mu = jnp.mean(r, axis=-1, keepdims=True)
    d = r - mu
    var = jnp.mean(d * d, axis=-1, keepdims=True)
    return d * lax.rsqrt(var + LN_EPS) * g + b


def _store_stream(r, x_ref, perm_refs, slab_ref):
    x_ref[...] = r
    if not perm_refs:
        return
    perm_refs[0][...] = r.astype(BF16)
    tm = r.shape[0]
    for s in range(N_SLABS):
        slab_ref[s] = r[:, s * LANES:(s + 1) * LANES]
    for ref, d in zip(perm_refs[1:], DILATIONS[1:]):
        for res in range(d):
            for s in range(N_SLABS):
                ref[res, :, s * LANES:(s + 1) * LANES] = (
                    slab_ref[s, pl.ds(res, tm // d, stride=d), :].astype(BF16))


def _stream_out(tm, with_perm):
    per_seq = SEQ // tm
    shapes = [jax.ShapeDtypeStruct((TOKENS, D_MODEL), F32)]
    specs = [pl.BlockSpec((tm, D_MODEL), lambda i: (i, 0))]
    if with_perm:
        shapes.append(jax.ShapeDtypeStruct((TOKENS, D_MODEL), BF16))
        specs.append(pl.BlockSpec((tm, D_MODEL), lambda i: (i, 0)))
        for d in DILATIONS[1:]:
            shapes.append(jax.ShapeDtypeStruct((BATCH, d, SEQ // d, D_MODEL), BF16))
            specs.append(pl.BlockSpec((None, d, tm // d, D_MODEL),
                                      lambda i: (i // per_seq, 0, i % per_seq, 0)))
    return shapes, specs


def _proj_kernel(x_ref, w_ref, o_ref, *, out_scale):
    acc = _dot(x_ref[...].astype(BF16), w_ref[...].astype(BF16))
    if out_scale is not None:
        acc = acc * out_scale
    o_ref[...] = acc.astype(o_ref.dtype)


def _proj(x, w, layer, col0, n, name, out_scale=None, tm=1024, tn=1024):
    m, k = x.shape
    col_blk0 = col0 // tn
    return pl.pallas_call(
        functools.partial(_proj_kernel, out_scale=out_scale),
        grid=(m // tm, n // tn),
        in_specs=[pl.BlockSpec((tm, k), lambda i, j: (i, 0)),
                  pl.BlockSpec((None, k, tn), lambda i, j: (layer, 0, col_blk0 + j))],
        out_specs=pl.BlockSpec((tm, tn), lambda i, j: (i, j)),
        out_shape=jax.ShapeDtypeStruct((m, n), BF16),
        compiler_params=pltpu.CompilerParams(dimension_semantics=("parallel", "arbitrary"),
                                             vmem_limit_bytes=VMEM_LIMIT),
        name=name,
    )(x, w)


def _pre_kernel(xm_ref, halo_ref, cw_ref, cb_ref, bdqk_ref, bdv_ref, wgc_ref, wgm_ref, bg_ref,
                q_ref, k_ref, v_ref, xc_ref, wi_ref, cv_ref, ext_ref, m_ref):
    tm = xm_ref.shape[0]
    L = CHUNK_A
    scale = HEAD_DIM_A ** -0.5
    i = pl.program_id(0)
    seq_start = (i * tm) % SEQ == 0

    @pl.when(seq_start)
    def _():
        m_ref[...] = jnp.zeros_like(m_ref)

    halo = jnp.where(seq_start, 0.0, halo_ref[...].astype(F32))
    ext_ref[0:SUBLANES, :] = halo
    ext_ref[SUBLANES:, :] = xm_ref[...].astype(F32)
    acc = cb_ref[...] + cw_ref[CONV_K - 1:CONV_K, :] * ext_ref[SUBLANES:, :]
    for j in range(CONV_K - 1):
        off = SUBLANES - (CONV_K - 1) + j
        acc = acc + cw_ref[j:j + 1, :] * ext_ref[pl.ds(off, tm), :]
    xc_ref[...] = (acc * _sigmoid(acc)).astype(BF16)

    for blk in range(N_LANE_BLOCKS_A):
        sl = slice(blk * LANES, (blk + 1) * LANES)
        qk = _dot(xc_ref[:, sl], bdqk_ref[blk])
        q_ref[:, sl] = qk[:, :LANES].astype(BF16)
        k_ref[:, sl] = qk[:, LANES:].astype(BF16)
        v_ref[:, sl] = _dot(xm_ref[:, sl], bdv_ref[blk]).astype(BF16)

    gates = _dot(xc_ref[...], wgc_ref[...]) + _dot(xm_ref[...], wgm_ref[...]) + bg_ref[...]
    log_sig = jnp.minimum(gates, 0.0) - jnp.log(1.0 + jnp.exp(-jnp.abs(gates)))
    row = lax.broadcasted_iota(jnp.int32, (L, L), 0)
    col = lax.broadcasted_iota(jnp.int32, (L, L), 1)
    tri = col <= row
    lane = lax.broadcasted_iota(jnp.int32, (SUBLANES, L), 1)
    cv_ref[...] = jnp.zeros_like(cv_ref)
    for c in range(tm // L):
        rows = slice(c * L, (c + 1) * L)
        g_cols = gates[rows, :]
        li_rows = g_cols.T[0:SUBLANES, :]
        lf_rows = log_sig[rows, :].T[0:SUBLANES, :]
        b_rows = lf_rows
        shift = 1
        while shift < L:
            b_rows = b_rows + jnp.where(lane >= shift, pltpu.roll(b_rows, shift, 1), 0.0)
            shift *= 2
        for h in range(HEADS_A):
            li_row = li_rows[h:h + 1, :]
            lf_row = lf_rows[HEADS_A + h:HEADS_A + h + 1, :]
            b_row = b_rows[HEADS_A + h:HEADS_A + h + 1, :]
            li_col = g_cols[:, h:h + 1]
            b_col = jnp.sum(jnp.where(tri, lf_row, 0.0), axis=1, keepdims=True)
            g_tot = jnp.sum(lf_row, axis=1, keepdims=True)
            m_prev = m_ref[h:h + 1, 0:1]
            d_intra = jnp.where(tri, b_col - b_row + li_row, -jnp.inf)
            a_inter = b_col + m_prev
            m_t = jnp.maximum(a_inter, jnp.max(d_intra, axis=1, keepdims=True))
            wi_ref[h, rows, :] = jnp.exp(d_intra - m_t) * scale
            lw_col = g_tot - b_col + li_col
            m_new = jnp.maximum(g_tot + m_prev, jnp.max(lw_col, axis=0, keepdims=True))
            cv_ref[rows, CV_W_INTER + h:CV_W_INTER + h + 1] = jnp.exp(a_inter - m_t)
            cv_ref[rows, CV_FLOOR + h:CV_FLOOR + h + 1] = jnp.exp(-m_t)
            cv_ref[rows, CV_W_STATE + h:CV_W_STATE + h + 1] = jnp.exp(lw_col - m_new) * scale
            cv_ref[rows, CV_DECAY + h:CV_DECAY + h + 1] = jnp.broadcast_to(
                jnp.exp(g_tot + m_prev - m_new), (L, 1))
            m_ref[h:h + 1, 0:1] = m_new


def _mlstm_pre(proj, cw, cb, bdqk, bdv, wgc, wgm, bg, name, tm=512):
    n_rows = proj.shape[0]
    act = jax.ShapeDtypeStruct((n_rows, INNER_A), BF16)
    row_spec = pl.BlockSpec((tm, INNER_A), lambda i: (i, 0))
    full = lambda shape: pl.BlockSpec(shape, lambda i: (0,) * len(shape))
    return pl.pallas_call(
        _pre_kernel,
        grid=(n_rows // tm,),
        in_specs=[row_spec,
                  pl.BlockSpec((SUBLANES, INNER_A),
                               lambda i: (jnp.maximum(i * (tm // SUBLANES) - 1, 0), 0)),
                  full((CONV_K, INNER_A)), full((1, INNER_A)),
                  full((N_LANE_BLOCKS_A, LANES, 2 * LANES)), full((N_LANE_BLOCKS_A, LANES, LANES)),
                  full((INNER_A, LANES)), full((INNER_A, LANES)), full((1, LANES))],
        out_specs=[row_spec, row_spec, row_spec, row_spec,
                   pl.BlockSpec((HEADS_A, tm, CHUNK_A), lambda i: (0, i, 0)),
                   pl.BlockSpec((tm, LANES), lambda i: (i, 0))],
        out_shape=[act, act, act, act,
                   jax.ShapeDtypeStruct((HEADS_A, n_rows, CHUNK_A), F32),
                   jax.ShapeDtypeStruct((n_rows, LANES), F32)],
        scratch_shapes=[pltpu.VMEM((tm + SUBLANES, INNER_A), F32), pltpu.VMEM((SUBLANES, LANES), F32)],
        compiler_params=pltpu.CompilerParams(dimension_semantics=("arbitrary",),
                                             vmem_limit_bytes=VMEM_LIMIT),
        name=name,
    )(proj, proj, cw, cb, bdqk, bdv, wgc, wgm, bg)


def _mlstm_kernel(q_ref, k_ref, v_ref, xc_ref, z_ref, o_ref, wi_ref, cv_ref, gn_ref, sk_ref, y_ref,
                  ct_ref, n_ref):
    @pl.when(pl.program_id(1) == 0)
    def _():
        ct_ref[...] = jnp.zeros_like(ct_ref)
        n_ref[...] = jnp.zeros_like(n_ref)

    for h in range(HEADS_A):
        hs = slice(h * HEAD_DIM_A, (h + 1) * HEAD_DIM_A)
        qb = q_ref[:, hs]
        kb = k_ref[:, hs]
        vb = v_ref[:, hs]
        w_inter = cv_ref[:, CV_W_INTER + h:CV_W_INTER + h + 1]
        floor = cv_ref[:, CV_FLOOR + h:CV_FLOOR + h + 1]
        w_state = cv_ref[:, CV_W_STATE + h:CV_W_STATE + h + 1]
        decay = cv_ref[0:1, CV_DECAY + h:CV_DECAY + h + 1]

        sc = _dot_nt(qb, kb) * wi_ref[h]
        ct = ct_ref[h]
        num = w_inter * _dot(qb, ct.astype(BF16)) + _dot(sc.astype(BF16), vb)
        den = (w_inter * jnp.sum(qb.astype(F32) * n_ref[h], axis=1, keepdims=True)
               + jnp.sum(sc, axis=1, keepdims=True))
        h_tilde = num * (1.0 / jnp.maximum(jnp.abs(den), floor))

        kw = kb.astype(F32) * w_state
        ct_ref[h] = decay * ct + _dot(kw.T.astype(BF16), vb)
        n_ref[h] = decay * n_ref[h] + jnp.sum(kw, axis=0, keepdims=True)

        hg = h_tilde * _sigmoid(o_ref[:, hs].astype(F32))
        mu = jnp.mean(hg, axis=1, keepdims=True)
        dlt = hg - mu
        var = jnp.mean(dlt * dlt, axis=1, keepdims=True)
        hn = dlt * lax.rsqrt(var + LN_EPS) * gn_ref[:, hs]
        zz = z_ref[:, hs].astype(F32)
        y = (hn + sk_ref[:, hs] * xc_ref[:, hs].astype(F32)) * (zz * _sigmoid(zz))
        y_ref[:, hs] = y.astype(BF16)


def _mlstm_core(q, k, v, xc, proj, wi, cv, gn, skip, name):
    L = CHUNK_A
    dh = HEAD_DIM_A
    row = lambda b, c: b * N_CHUNKS_A + c
    act_spec = pl.BlockSpec((L, INNER_A), lambda b, c: (row(b, c), 0))
    vec_spec = pl.BlockSpec((1, INNER_A), lambda b, c: (0, 0))
    return pl.pallas_call(
        _mlstm_kernel,
        grid=(BATCH, N_CHUNKS_A),
        in_specs=[act_spec, act_spec, act_spec, act_spec,
                  pl.BlockSpec((L, INNER_A), lambda b, c: (row(b, c), 1)),
                  pl.BlockSpec((L, INNER_A), lambda b, c: (row(b, c), 2)),
                  pl.BlockSpec((HEADS_A, L, L), lambda b, c: (0, row(b, c), 0)),
                  pl.BlockSpec((L, LANES), lambda b, c: (row(b, c), 0)),
                  vec_spec, vec_spec],
        out_specs=act_spec,
        out_shape=jax.ShapeDtypeStruct((TOKENS, INNER_A), BF16),
        scratch_shapes=[pltpu.VMEM((HEADS_A, dh, dh), F32), pltpu.VMEM((HEADS_A, 1, dh), F32)],
        compiler_params=pltpu.CompilerParams(
            dimension_semantics=("parallel", "arbitrary"), vmem_limit_bytes=VMEM_LIMIT),
        name=name,
    )(q, k, v, xc, proj, proj, wi, cv, gn, skip)


def _out_ln_kernel(y_ref, w_ref, x_ref, g_ref, b_ref, *rest, with_perm):
    n_out = 1 + (N_GROUPS_B if with_perm else 0)
    outs, scratch = rest[:n_out], rest[n_out:]
    r = ALPHA * x_ref[...] + _dot(y_ref[...], w_ref[...])
    _store_stream(_layer_norm_rows(r, g_ref[...], b_ref[...]), outs[0], outs[1:],
                  scratch[0] if with_perm else None)


def _out_ln(y, w, x, g, b, name, with_perm, tm=512):
    m, k = y.shape
    n = w.shape[1]
    out_shape, out_specs = _stream_out(tm, with_perm)
    return pl.pallas_call(
        functools.partial(_out_ln_kernel, with_perm=with_perm),
        grid=(m // tm,),
        in_specs=[pl.BlockSpec((tm, k), lambda i: (i, 0)),
                  pl.BlockSpec((k, n), lambda i: (0, 0)),
                  pl.BlockSpec((tm, n), lambda i: (i, 0)),
                  pl.BlockSpec((1, n), lambda i: (0, 0)),
                  pl.BlockSpec((1, n), lambda i: (0, 0))],
        out_specs=out_specs,
        out_shape=out_shape,
        scratch_shapes=[pltpu.VMEM((N_SLABS, tm, LANES), F32)] if with_perm else [],
        compiler_params=pltpu.CompilerParams(dimension_semantics=("parallel",),
                                             vmem_limit_bytes=VMEM_LIMIT),
        name=name,
    )(y, w, x, g, b)


def _attn_kernel(q_ref, kvp_ref, kvc_ref, o_ref, lse_ref, *, dilation, blocks_per_seq):
    n = BLOCK_B
    n_blocks = q_ref.shape[0] // n
    qi = lax.broadcasted_iota(jnp.int32, (n, n), 0)
    kj = lax.broadcasted_iota(jnp.int32, (n, n), 1)
    lower = kj <= qi
    diag = kj == qi
    dist = jnp.bitwise_and(qi - kj, n - 1).astype(F32)
    heads = range(HEADS_B)
    ksl = [slice(h * HEAD_DIM_B, (h + 1) * HEAD_DIM_B) for h in heads]
    vsl = [slice(INNER_B + h * HEAD_DIM_B, INNER_B + (h + 1) * HEAD_DIM_B) for h in heads]
    slopes = [2.0 ** (-8.0 * (h + 1.0) / HEADS_B) * dilation for h in heads]

    for blk in range(n_blocks):
        rows = slice(blk * n, (blk + 1) * n)
        prev_ref, prev_rows = ((kvp_ref, slice(0, n)) if blk == 0
                               else (kvc_ref, slice((blk - 1) * n, blk * n)))
        first = (pl.program_id(0) * n_blocks + blk) % blocks_per_seq == 0
        prev_bias = jnp.where(first, -jnp.inf, 0.0)

        scores = []
        for h in heads:
            qh = q_ref[rows, ksl[h]]
            s_prev = _dot_nt(qh, prev_ref[prev_rows, ksl[h]]) + prev_bias
            s = jnp.where(lower, _dot_nt(qh, kvc_ref[rows, ksl[h]]), s_prev) - slopes[h] * dist
            s_diag = (jnp.max(jnp.where(diag, s_prev, -jnp.inf), axis=-1, keepdims=True)
                      - slopes[h] * float(n))
            m = jnp.maximum(jnp.max(s, axis=-1, keepdims=True), s_diag)
            scores.append((s, s_diag, m))
        probs = []
        for h in heads:
            s, s_diag, m = scores[h]
            e = jnp.exp(s - m)
            e_diag = jnp.exp(s_diag - m)
            den = jnp.sum(e, axis=-1, keepdims=True) + e_diag
            probs.append((jnp.where(lower, e, 0.0).astype(BF16), jnp.where(lower, 0.0, e).astype(BF16),
                          e_diag, den))
        lse_all = jnp.zeros((n, LANES), F32)
        for h in heads:
            p_cur, p_prev, e_diag, den = probs[h]
            vp = prev_ref[prev_rows, vsl[h]]
            o = _dot(p_cur, kvc_ref[rows, vsl[h]]) + _dot(p_prev, vp) + e_diag * vp.astype(F32)
            o_ref[rows, ksl[h]] = (o * (1.0 / den)).astype(BF16)
            lse_all = jnp.where(kj == h, scores[h][2] + jnp.log(den), lse_all)
        lse_ref[rows, :] = lse_all


def _attn_group(q, kv, group, name, blocks_per_step=4):
    dilation = DILATIONS[group]
    blocks_per_seq = SEQ // dilation // BLOCK_B
    tq = blocks_per_step * BLOCK_B
    return pl.pallas_call(
        functools.partial(_attn_kernel, dilation=float(dilation), blocks_per_seq=blocks_per_seq),
        grid=(TOKENS // tq,),
        in_specs=[pl.BlockSpec((tq, INNER_B), lambda n: (n, 0)),
                  pl.BlockSpec((BLOCK_B, 2 * INNER_B),
                               lambda n: (jnp.maximum(n * blocks_per_step - 1, 0), 0)),
                  pl.BlockSpec((tq, 2 * INNER_B), lambda n: (n, 0))],
        out_specs=[pl.BlockSpec((tq, INNER_B), lambda n: (n, 0)),
                   pl.BlockSpec((tq, LANES), lambda n: (n, 0))],
        out_shape=[jax.ShapeDtypeStruct((TOKENS, INNER_B), BF16),
                   jax.ShapeDtypeStruct((TOKENS, LANES), F32)],
        compiler_params=pltpu.CompilerParams(dimension_semantics=("arbitrary",),
                                             vmem_limit_bytes=VMEM_LIMIT),
        name=name,
    )(q, kv, kv)


def _merge_out_ln_kernel(o0_ref, o1_ref, o2_ref, l0_ref, l1_ref, l2_ref, z_ref, w_ref, x_ref,
                         g_ref, b_ref, *rest, with_perm):
    n_out = 1 + (N_GROUPS_B if with_perm else 0)
    outs = rest[:n_out]
    y_ref, os1_ref, os2_ref, ls1_ref, ls2_ref = rest[n_out:]
    tm = x_ref.shape[0]
    for o_ref, l_ref, os_ref, ls_ref, d in ((o1_ref, l1_ref, os1_ref, ls1_ref, DILATIONS[1]),
                                            (o2_ref, l2_ref, os2_ref, ls2_ref, DILATIONS[2])):
        for res in range(d):
            ls_ref[pl.ds(res, tm // d, stride=d), :] = l_ref[res]
            for s in range(N_SLABS):
                os_ref[s, pl.ds(res, tm // d, stride=d), :] = (
                    o_ref[res, :, s * LANES:(s + 1) * LANES].astype(F32))
    l0, l1, l2 = l0_ref[...], ls1_ref[...], ls2_ref[...]
    mx = jnp.maximum(jnp.maximum(l0, l1), l2)
    e0, e1, e2 = jnp.exp(l0 - mx), jnp.exp(l1 - mx), jnp.exp(l2 - mx)
    inv = 1.0 / (e0 + e1 + e2)
    w0, w1, w2 = e0 * inv, e1 * inv, e2 * inv
    for h in range(HEADS_B):
        sl = slice(h * HEAD_DIM_B, (h + 1) * HEAD_DIM_B)
        o = (w0[:, h:h + 1] * o0_ref[:, sl].astype(F32) + w1[:, h:h + 1] * os1_ref[h]
             + w2[:, h:h + 1] * os2_ref[h])
        zz = z_ref[:, sl].astype(F32)
        y_ref[:, sl] = (o * (zz * _sigmoid(zz))).astype(BF16)
    r = ALPHA * x_ref[...] + _dot(y_ref[...], w_ref[...])
    _store_stream(_layer_norm_rows(r, g_ref[...], b_ref[...]), outs[0], outs[1:], os1_ref)


def _merge_out_ln(os_, lses, z, w, x, g, b, name, with_perm, tm=512):
    m = x.shape[0]
    n = D_MODEL
    per_seq = SEQ // tm
    tok = lambda width: pl.BlockSpec((tm, width), lambda i: (i, 0))
    res = lambda d, width: pl.BlockSpec((None, d, tm // d, width),
                                        lambda i: (i // per_seq, 0, i % per_seq, 0))
    vec = pl.BlockSpec((1, n), lambda i: (0, 0))
    d1, d2 = DILATIONS[1], DILATIONS[2]
    out_shape, out_specs = _stream_out(tm, with_perm)
    return pl.pallas_call(
        functools.partial(_merge_out_ln_kernel, with_perm=with_perm),
        grid=(m // tm,),
        in_specs=[tok(INNER_B), res(d1, INNER_B), res(d2, INNER_B),
                  tok(LANES), res(d1, LANES), res(d2, LANES),
                  tok(INNER_B),
                  pl.BlockSpec((INNER_B, n), lambda i: (0, 0)),
                  tok(n), vec, vec],
        out_specs=out_specs,
        out_shape=out_shape,
        scratch_shapes=[pltpu.VMEM((tm, INNER_B), BF16),
                        pltpu.VMEM((N_SLABS, tm, LANES), F32), pltpu.VMEM((N_SLABS, tm, LANES), F32),
                        pltpu.VMEM((tm, LANES), F32), pltpu.VMEM((tm, LANES), F32)],
        compiler_params=pltpu.CompilerParams(dimension_semantics=("parallel",),
                                             vmem_limit_bytes=VMEM_LIMIT),
        name=name,
    )(os_[0], os_[1].reshape(BATCH, d1, SEQ // d1, INNER_B), os_[2].reshape(BATCH, d2, SEQ // d2, INNER_B),
      lses[0], lses[1].reshape(BATCH, d1, SEQ // d1, LANES), lses[2].reshape(BATCH, d2, SEQ // d2, LANES),
      z, w, x, g, b)


def _block_diag_lane_blocks(w):
    per = LANES // QKV_BLOCK
    w4 = w.reshape(-1, per, QKV_BLOCK, QKV_BLOCK)
    eye = jnp.eye(per, dtype=w.dtype)
    dense = jnp.einsum('bmij,mp->bmjpi', w4, eye)
    return dense.reshape(-1, LANES, LANES)


def _fold_gate_weights(wq, wk, wv, w_if):
    nblk = INNER_A // QKV_BLOCK
    wif = w_if.reshape(3, nblk, QKV_BLOCK, 2 * HEADS_A)
    hp = lax.Precision.HIGHEST
    wgc = (jnp.einsum('nij,nio->njo', wq, wif[0], precision=hp)
           + jnp.einsum('nij,nio->njo', wk, wif[1], precision=hp)).reshape(INNER_A, 2 * HEADS_A)
    wgm = jnp.einsum('nij,nio->njo', wv, wif[2], precision=hp).reshape(INNER_A, 2 * HEADS_A)
    pad = ((0, 0), (0, LANES - 2 * HEADS_A))
    return jnp.pad(wgc, pad).astype(BF16), jnp.pad(wgm, pad).astype(BF16)


def kernel(x, ln_g, ln_b, a_w_in, a_conv_w, a_conv_b, a_wq, a_wk, a_wv, a_w_if, a_b_if, a_gn_g, a_skip,
           a_w_out, b_w_kv, b_w_in, b_w_out):
    xs = x.reshape(TOKENS, D_MODEL)
    x_in = xs
    for layer in range(N_A_LAYERS):
        bdqk =jnp.concatenate([_block_diag_lane_blocks(a_wq[layer]),
                                _block_diag_lane_blocks(a_wk[layer])], axis=-1).astype(BF16)
        bdv = _block_diag_lane_blocks(a_wv[layer]).astype(BF16)
        wgc, wgm = _fold_gate_weights(a_wq[layer], a_wk[layer], a_wv[layer], a_w_if[layer])
        bg = jnp.pad(a_b_if[layer][None, :], ((0, 0), (0, LANES - 2 * HEADS_A)))
        proj = _proj(x_in, a_w_in, layer, 0, 3 * INNER_A, f"a{layer}_proj")
        q, k, v, xc, wi, cv = _mlstm_pre(proj, a_conv_w[layer], a_conv_b[layer][None, :], bdqk, bdv,
                                         wgc, wgm, bg, f"a{layer}_pre")
        y = _mlstm_core(q, k, v, xc, proj, wi, cv, a_gn_g[layer][None, :], a_skip[layer][None, :],
                        f"a{layer}_mlstm")
        outs = _out_ln(y, a_w_out[layer].astype(BF16), xs, ln_g[layer][None, :], ln_b[layer][None, :],
                       f"a{layer}_out_ln", with_perm=layer == N_A_LAYERS - 1)
        xs = outs[0]
        x_in = xs

    perms = [p.reshape(TOKENS, D_MODEL) for p in outs[1:]]
    kvs = [_proj(perms[g], b_w_kv[None], 0, 2 * g * INNER_B, 2 * INNER_B, f"kv_proj{g}")
           for g in range(N_GROUPS_B)]
    q_scale = HEAD_DIM_B ** -0.5
    for lb in range(N_B_LAYERS):
        layer = N_A_LAYERS + lb
        qs = [_proj(perms[g], b_w_in, lb, g * INNER_B, INNER_B, f"b{lb}_proj{g}", out_scale=q_scale)
              for g in range(N_GROUPS_B)]
        z = _proj(perms[0], b_w_in, lb, N_GROUPS_B * INNER_B, INNER_B, f"b{lb}_projz")
        os_, lses = [], []
        for g in range(N_GROUPS_B):
            o, lse = _attn_group(qs[g], kvs[g], g, f"b{lb}_attn{g}")
            os_.append(o)
            lses.append(lse)
        outs = _merge_out_ln(os_, lses, z, b_w_out[lb].astype(BF16), xs, ln_g[layer][None, :],
                             ln_b[layer][None, :], f"b{lb}_merge_out_ln",
                             with_perm=lb < N_B_LAYERS - 1)
        xs = outs[0]
        perms = [p.reshape(TOKENS, D_MODEL) for p in outs[1:]]
    return xs.reshape(BATCH, SEQ, D_MODEL)
```

```python
import functools

import jax
import jax.numpy as jnp
from jax import lax
from jax.experimental import pallas as pl
from jax.experimental.pallas import tpu as pltpu

D_MODEL = 1024
BATCH = 4
SEQ = 4096
DEPTH = 4
N_A_LAYERS = DEPTH // 2
N_B_LAYERS = DEPTH - N_A_LAYERS
INNER_A = 2 * D_MODEL
HEADS_A = 4
HEAD_DIM_A = INNER_A // HEADS_A
QKV_BLOCK = 4
CONV_K = 4
HEAD_DIM_B = 128
HEADS_B = D_MODEL // HEAD_DIM_B
INNER_B = HEADS_B * HEAD_DIM_B
GROUPS_B = ((128, 1), (512, 4), (2048, 16))
N_GROUPS_B = len(GROUPS_B)
DILATIONS = tuple(d for _, d in GROUPS_B)
BLOCK_B = 128
ALPHA = (2 * DEPTH) ** 0.25
LN_EPS = 1e-5

TOKENS = BATCH * SEQ
LANES = 128
SUBLANES = 8
N_SLABS = D_MODEL // LANES
CHUNK_A = 128
N_LANE_BLOCKS_A = INNER_A // LANES
VMEM_LIMIT = 48 * 1024 * 1024
PROJ_TN = 1024
CV_W_INTER, CV_FLOOR, CV_W_STATE, CV_DECAY = 0, HEADS_A, 2 * HEADS_A, 3 * HEADS_A

F32 = jnp.float32
BF16 = jnp.bfloat16

assert all(w // d == BLOCK_B for w, d in GROUPS_B)


def _dot(a, b):
    return jnp.dot(a, b, preferred_element_type=F32)


def _dot_nt(a, b):
    return lax.dot_general(a, b, (((1,), (1,)), ((), ())), preferred_element_type=F32)


def _sigmoid(x):
    return 1.0 / (1.0 + jnp.exp(-x))


def _layer_norm_rows(r, g, b):
    mu = jnp.mean(r, axis=-1, keepdims=True)
    d = r - mu
    var = jnp.mean(d * d, axis=-1, keepdims=True)
    return d * lax.rsqrt(var + LN_EPS) * g + b


def _store_stream(r, x_ref, perm_refs, slab_ref):
    x_ref[...] = r
    if not perm_refs:
        return
    perm_refs[0][...] = r.astype(BF16)
    tm = r.shape[0]
    for s in range(N_SLABS):
        slab_ref[s] = r[:, s * LANES:(s + 1) * LANES]
    for ref, d in zip(perm_refs[1:], DILATIONS[1:]):
        for res in range(d):
            for s in range(N_SLABS):
                ref[res, :, s * LANES:(s + 1) * LANES] = (
                    slab_ref[s, pl.ds(res, tm // d, stride=d), :].astype(BF16))


def _stream_out(tm, with_perm):
    per_seq = SEQ // tm
    shapes = [jax.ShapeDtypeStruct((TOKENS, D_MODEL), F32)]
    specs = [pl.BlockSpec((tm, D_MODEL), lambda i: (i, 0))]
    if with_perm:
        shapes.append(jax.ShapeDtypeStruct((TOKENS, D_MODEL), BF16))
        specs.append(pl.BlockSpec((tm, D_MODEL), lambda i: (i, 0)))
        for d in DILATIONS[1:]:
            shapes.append(jax.ShapeDtypeStruct((BATCH, d, SEQ // d, D_MODEL), BF16))
            specs.append(pl.BlockSpec((None, d, tm // d, D_MODEL),
                                      lambda i: (i // per_seq, 0, i % per_seq, 0)))
    return shapes, specs


def _proj_kernel(x_ref, w_ref, o_ref, wb_ref, *, out_scale):
    @pl.when(pl.program_id(1) == 0)
    def _():
        wb_ref[...] = w_ref[...].astype(BF16)

    acc = _dot(x_ref[...].astype(BF16), wb_ref[...])
    if out_scale is not None:
        acc = acc * out_scale
    o_ref[...] = acc.astype(o_ref.dtype)


def _proj(x, w, layer, col0, n, name, out_scale=None, tm=2048, tn=PROJ_TN):
    m, k = x.shape
    col_blk0 = col0 // tn
    return pl.pallas_call(
        functools.partial(_proj_kernel, out_scale=out_scale),
        grid=(n // tn, m // tm),
        in_specs=[pl.BlockSpec((tm, k), lambda j, i: (i, 0)),
                  pl.BlockSpec((None, k, tn), lambda j, i: (layer, 0, col_blk0 + j))],
        out_specs=pl.BlockSpec((tm, tn), lambda j, i: (i, j)),
        out_shape=jax.ShapeDtypeStruct((m, n), BF16),
        scratch_shapes=[pltpu.VMEM((k, tn), BF16)],
        compiler_params=pltpu.CompilerParams(dimension_semantics=("parallel", "arbitrary"),
                                             vmem_limit_bytes=VMEM_LIMIT),
        name=name,
    )(x, w)


def _pre_kernel(xm_ref, halo_ref, cw_ref, cb_ref, bdqk_ref, bdv_ref, wgc_ref, wgm_ref, bg_ref,
                q_ref, k_ref, v_ref, xc_ref, wi_ref, cv_ref, ext_ref, m_ref):
    tm = xm_ref.shape[0]
    L = CHUNK_A
    scale = HEAD_DIM_A ** -0.5
    i = pl.program_id(0)
    seq_start = (i * tm) % SEQ == 0

    @pl.when(seq_start)
    def _():
        m_ref[...] = jnp.zeros_like(m_ref)

    halo = jnp.where(seq_start, 0.0, halo_ref[...].astype(F32))
    ext_ref[0:SUBLANES, :] = halo
    ext_ref[SUBLANES:, :] = xm_ref[...].astype(F32)
    acc = cb_ref[...] + cw_ref[CONV_K - 1:CONV_K, :] * ext_ref[SUBLANES:, :]
    for j in range(CONV_K - 1):
        off = SUBLANES - (CONV_K - 1) + j
        acc = acc + cw_ref[j:j + 1, :] * ext_ref[pl.ds(off, tm), :]
    xc_ref[...] = (acc * _sigmoid(acc)).astype(BF16)

    for blk in range(N_LANE_BLOCKS_A):
        sl = slice(blk * LANES, (blk + 1) * LANES)
        qk = _dot(xc_ref[:, sl], bdqk_ref[blk])
        q_ref[:, sl] = qk[:, :LANES].astype(BF16)
        k_ref[:, sl] = qk[:, LANES:].astype(BF16)
        v_ref[:, sl] = _dot(xm_ref[:, sl], bdv_ref[blk]).astype(BF16)

    gates = _dot(xc_ref[...], wgc_ref[...]) + _dot(xm_ref[...], wgm_ref[...]) + bg_ref[...]
    log_sig = jnp.minimum(gates, 0.0) - jnp.log(1.0 + jnp.exp(-jnp.abs(gates)))
    row = lax.broadcasted_iota(jnp.int32, (L, L), 0)
    col = lax.broadcasted_iota(jnp.int32, (L, L), 1)
    tri = col <= row
    lane = lax.broadcasted_iota(jnp.int32, (SUBLANES, L), 1)
    cv_ref[...] = jnp.zeros_like(cv_ref)
    for c in range(tm // L):
        rows = slice(c * L, (c + 1) * L)
        g_cols = gates[rows, :]
        li_rows = g_cols.T[0:SUBLANES, :]
        lf_rows = log_sig[rows, :].T[0:SUBLANES, :]
        b_rows = lf_rows
        shift = 1
        while shift < L:
            b_rows = b_rows + jnp.where(lane >= shift, pltpu.roll(b_rows, shift, 1), 0.0)
            shift *= 2
        for h in range(HEADS_A):
            li_row = li_rows[h:h + 1, :]
            lf_row = lf_rows[HEADS_A + h:HEADS_A + h + 1, :]
            b_row = b_rows[HEADS_A + h:HEADS_A + h + 1, :]
            li_col = g_cols[:, h:h + 1]
            b_col = jnp.sum(jnp.where(tri, lf_row, 0.0), axis=1, keepdims=True)
            g_tot = jnp.sum(lf_row, axis=1, keepdims=True)
            m_prev = m_ref[h:h + 1, 0:1]
            d_intra = jnp.where(tri, b_col - b_row + li_row, -jnp.inf)
            a_inter = b_col + m_prev
            m_t = jnp.maximum(a_inter, jnp.max(d_intra, axis=1, keepdims=True))
            wi_ref[h, rows, :] = jnp.exp(d_intra - m_t) * scale
            lw_col = g_tot - b_col + li_col
            m_new = jnp.maximum(g_tot + m_prev, jnp.max(lw_col, axis=0, keepdims=True))
            cv_ref[rows, CV_W_INTER + h:CV_W_INTER + h + 1] = jnp.exp(a_inter - m_t)
            cv_ref[rows, CV_FLOOR + h:CV_FLOOR + h + 1] = jnp.exp(-m_t)
            cv_ref[rows, CV_W_STATE + h:CV_W_STATE + h + 1] = jnp.exp(lw_col - m_new) * scale
            cv_ref[rows, CV_DECAY + h:CV_DECAY + h + 1] = jnp.broadcast_to(
                jnp.exp(g_tot + m_prev - m_new), (L, 1))
            m_ref[h:h + 1, 0:1] = m_new


def _mlstm_pre(proj, cw, cb, bdqk, bdv, wgc, wgm, bg, name, tm=512):
    n_rows = proj.shape[0]
    act = jax.ShapeDtypeStruct((n_rows, INNER_A), BF16)
    row_spec = pl.BlockSpec((tm, INNER_A), lambda i: (i, 0))
    full = lambda shape: pl.BlockSpec(shape, lambda i: (0,) * len(shape))
    return pl.pallas_call(
        _pre_kernel,
        grid=(n_rows // tm,),
        in_specs=[row_spec,
                  pl.BlockSpec((SUBLANES, INNER_A),
                               lambda i: (jnp.maximum(i * (tm // SUBLANES) - 1, 0), 0)),
                  full((CONV_K, INNER_A)), full((1, INNER_A)),
                  full((N_LANE_BLOCKS_A, LANES, 2 * LANES)), full((N_LANE_BLOCKS_A, LANES, LANES)),
                  full((INNER_A, LANES)), full((INNER_A, LANES)), full((1, LANES))],
        out_specs=[row_spec, row_spec, row_spec, row_spec,
                   pl.BlockSpec((HEADS_A, tm, CHUNK_A), lambda i: (0, i, 0)),
                   pl.BlockSpec((tm, LANES), lambda i: (i, 0))],
        out_shape=[act, act, act, act,
                   jax.ShapeDtypeStruct((HEADS_A, n_rows, CHUNK_A), F32),
                   jax.ShapeDtypeStruct((n_rows, LANES), F32)],
        scratch_shapes=[pltpu.VMEM((tm + SUBLANES, INNER_A), F32), pltpu.VMEM((SUBLANES, LANES), F32)],
        compiler_params=pltpu.CompilerParams(dimension_semantics=("arbitrary",),
                                             vmem_limit_bytes=VMEM_LIMIT),
        name=name,
    )(proj, proj, cw, cb, bdqk, bdv, wgc, wgm, bg)


def _mlstm_kernel(q_ref, k_ref, v_ref, xc_ref, z_ref, o_ref, wi_ref, cv_ref, gn_ref, sk_ref, y_ref,
                  ct_ref, n_ref):
    @pl.when(pl.program_id(1) == 0)
    def _():
        ct_ref[...] = jnp.zeros_like(ct_ref)
        n_ref[...] = jnp.zeros_like(n_ref)

    for c in range(q_ref.shape[0] // CHUNK_A):
        r = slice(c * CHUNK_A, (c + 1) * CHUNK_A)
        for h in range(HEADS_A):
            hs = slice(h * HEAD_DIM_A, (h + 1) * HEAD_DIM_A)
            qb = q_ref[r, hs]
            kb = k_ref[r, hs]
            vb = v_ref[r, hs]
            w_inter = cv_ref[r, CV_W_INTER + h:CV_W_INTER + h + 1]
            floor = cv_ref[r, CV_FLOOR + h:CV_FLOOR + h + 1]
            w_state = cv_ref[r, CV_W_STATE + h:CV_W_STATE + h + 1]
            decay = cv_ref[c * CHUNK_A:c * CHUNK_A + 1, CV_DECAY + h:CV_DECAY + h + 1]

            sc = _dot_nt(qb, kb) * wi_ref[h, r, :]
            ct = ct_ref[h]
            num = w_inter * _dot(qb, ct.astype(BF16)) + _dot(sc.astype(BF16), vb)
            den = (w_inter * jnp.sum(qb.astype(F32) * n_ref[h], axis=1, keepdims=True)
                   + jnp.sum(sc, axis=1, keepdims=True))
            h_tilde = num * (1.0 / jnp.maximum(jnp.abs(den), floor))

            kw = kb.astype(F32) * w_state
            ct_ref[h] = decay * ct + _dot(kw.T.astype(BF16), vb)
            n_ref[h] = decay * n_ref[h] + jnp.sum(kw, axis=0, keepdims=True)

            hg = h_tilde * _sigmoid(o_ref[r, hs].astype(F32))
            mu = jnp.mean(hg, axis=1, keepdims=True)
            dlt = hg - mu
            var = jnp.mean(dlt * dlt, axis=1, keepdims=True)
            hn = dlt * lax.rsqrt(var + LN_EPS) * gn_ref[:, hs]
            zz = z_ref[r, hs].astype(F32)
            y = (hn + sk_ref[:, hs] * xc_ref[r, hs].astype(F32)) * (zz * _sigmoid(zz))
            y_ref[r, hs] = y.astype(BF16)


def _mlstm_core(q, k, v, xc, proj, wi, cv, gn, skip, name, chunks_per_step=2):
    L = chunks_per_step * CHUNK_A
    dh = HEAD_DIM_A
    steps = SEQ // L
    row = lambda b, c: b * steps + c
    act_spec = pl.BlockSpec((L, INNER_A), lambda b, c: (row(b, c), 0))
    vec_spec = pl.BlockSpec((1, INNER_A), lambda b, c: (0, 0))
    return pl.pallas_call(
        _mlstm_kernel,
        grid=(BATCH, steps),
        in_specs=[act_spec, act_spec, act_spec, act_spec,
                  pl.BlockSpec((L, INNER_A), lambda b, c: (row(b, c), 1)),
                  pl.BlockSpec((L, INNER_A), lambda b, c: (row(b, c), 2)),
                  pl.BlockSpec((HEADS_A, L, CHUNK_A), lambda b, c: (0, row(b, c), 0)),
                  pl.BlockSpec((L, LANES), lambda b, c: (row(b, c), 0)),
                  vec_spec, vec_spec],
        out_specs=act_spec,
        out_shape=jax.ShapeDtypeStruct((TOKENS, INNER_A), BF16),
        scratch_shapes=[pltpu.VMEM((HEADS_A, dh, dh), F32), pltpu.VMEM((HEADS_A, 1, dh), F32)],
        compiler_params=pltpu.CompilerParams(
            dimension_semantics=("parallel", "arbitrary"), vmem_limit_bytes=VMEM_LIMIT),
        name=name,
    )(q, k, v, xc, proj, proj, wi, cv, gn, skip)


def _out_ln_kernel(y_ref, w_ref, x_ref, g_ref, b_ref, *rest, with_perm):
    n_out = 1 + (N_GROUPS_B if with_perm else 0)
    outs, scratch = rest[:n_out], rest[n_out:]
    r = ALPHA * x_ref[...] + _dot(y_ref[...], w_ref[...])
    _store_stream(_layer_norm_rows(r, g_ref[...], b_ref[...]), outs[0], outs[1:],
                  scratch[0] if with_perm else None)


def _out_ln(y, w, x, g, b, name, with_perm, tm=512):
    m, k = y.shape
    n = w.shape[1]
    out_shape, out_specs = _stream_out(tm, with_perm)
    return pl.pallas_call(
        functools.partial(_out_ln_kernel, with_perm=with_perm),
        grid=(m // tm,),
        in_specs=[pl.BlockSpec((tm, k), lambda i: (i, 0)),
                  pl.BlockSpec((k, n), lambda i: (0, 0)),
                  pl.BlockSpec((tm, n), lambda i: (i, 0)),
                  pl.BlockSpec((1, n), lambda i: (0, 0)),
                  pl.BlockSpec((1, n), lambda i: (0, 0))],
        out_specs=out_specs,
        out_shape=out_shape,
        scratch_shapes=[pltpu.VMEM((N_SLABS, tm, LANES), F32)] if with_perm else [],
        compiler_params=pltpu.CompilerParams(dimension_semantics=("parallel",),
                                             vmem_limit_bytes=VMEM_LIMIT),
        name=name,
    )(y, w, x, g, b)


def _attn_kernel(q_ref, kvp_ref, kvc_ref, o_ref, lse_ref, *, dilation, blocks_per_seq):
    n = BLOCK_B
    n_blocks = q_ref.shape[0] // n
    qi = lax.broadcasted_iota(jnp.int32, (n, n), 0)
    kj = lax.broadcasted_iota(jnp.int32, (n, n), 1)
    lower = kj <= qi
    diag = kj == qi
    dist = jnp.bitwise_and(qi - kj, n - 1).astype(F32)
    heads = range(HEADS_B)
    ksl = [slice(h * HEAD_DIM_B, (h + 1) * HEAD_DIM_B) for h in heads]
    vsl = [slice(INNER_B + h * HEAD_DIM_B, INNER_B + (h + 1) * HEAD_DIM_B) for h in heads]
    slopes = [2.0 ** (-8.0 * (h + 1.0) / HEADS_B) * dilation for h in heads]

    for blk in range(n_blocks):
        rows = slice(blk * n, (blk + 1) * n)
        prev_ref, prev_rows = ((kvp_ref, slice(0, n)) if blk == 0
                               else (kvc_ref, slice((blk - 1) * n, blk * n)))
        first = (pl.program_id(0) * n_blocks + blk) % blocks_per_seq == 0
        prev_bias = jnp.where(first, -jnp.inf, 0.0)

        both_rows = slice((blk - 1) * n, (blk + 1) * n)
        scores = []
        for h in heads:
            qh = q_ref[rows, ksl[h]]
            if blk == 0:
                s_prev = _dot_nt(qh, prev_ref[prev_rows, ksl[h]])
                s_cur = _dot_nt(qh, kvc_ref[rows, ksl[h]])
            else:
                s_both = _dot_nt(qh, kvc_ref[both_rows, ksl[h]])
                s_prev, s_cur = s_both[:, :n], s_both[:, n:]
            s_prev = s_prev + prev_bias
            s = jnp.where(lower, s_cur, s_prev) - slopes[h] * dist
            s_diag = (jnp.max(jnp.where(diag, s_prev, -jnp.inf), axis=-1, keepdims=True)
                      - slopes[h] * float(n))
            m = jnp.maximum(jnp.max(s, axis=-1, keepdims=True), s_diag)
            scores.append((s, s_diag, m))
        probs = []
        for h in heads:
            s, s_diag, m = scores[h]
            e = jnp.exp(s - m)
            e_diag = jnp.exp(s_diag - m)
            den = jnp.sum(e, axis=-1, keepdims=True) + e_diag
            probs.append((jnp.where(lower, e, 0.0).astype(BF16), jnp.where(lower, 0.0, e).astype(BF16),
                          e_diag, den))
        lse_all = jnp.zeros((n, LANES), F32)
        for h in heads:
            p_cur, p_prev, e_diag, den = probs[h]
            vp = prev_ref[prev_rows, vsl[h]]
            if blk == 0:
                o = _dot(p_cur, kvc_ref[rows, vsl[h]]) + _dot(p_prev, vp)
            else:
                o = _dot(jnp.concatenate([p_prev, p_cur], axis=1), kvc_ref[both_rows, vsl[h]])
            o = o + e_diag * vp.astype(F32)
            o_ref[rows, ksl[h]] = (o * (1.0 / den)).astype(BF16)
            lse_all = jnp.where(kj == h, scores[h][2] + jnp.log(den), lse_all)
        lse_ref[rows, :] = lse_all


def _attn_group(q, kv, group, name, blocks_per_step=4):
    dilation = DILATIONS[group]
    blocks_per_seq = SEQ // dilation // BLOCK_B
    tq = blocks_per_step * BLOCK_B
    return pl.pallas_call(
        functools.partial(_attn_kernel, dilation=float(dilation), blocks_per_seq=blocks_per_seq),
        grid=(TOKENS // tq,),
        in_specs=[pl.BlockSpec((tq, INNER_B), lambda n: (n, 0)),
                  pl.BlockSpec((BLOCK_B, 2 * INNER_B),
                               lambda n: (jnp.maximum(n * blocks_per_step - 1, 0), 0)),
                  pl.BlockSpec((tq, 2 * INNER_B), lambda n: (n, 0))],
        out_specs=[pl.BlockSpec((tq, INNER_B), lambda n: (n, 0)),
                   pl.BlockSpec((tq, LANES), lambda n: (n, 0))],
        out_shape=[jax.ShapeDtypeStruct((TOKENS, INNER_B), BF16),
                   jax.ShapeDtypeStruct((TOKENS, LANES), F32)],
        compiler_params=pltpu.CompilerParams(dimension_semantics=("arbitrary",),
                                             vmem_limit_bytes=VMEM_LIMIT),
        name=name,
    )(q, kv, kv)


def _merge_out_ln_kernel(o0_ref, o1_ref, o2_ref, l0_ref, l1_ref, l2_ref, z_ref, w_ref, x_ref,
                         g_ref, b_ref, *rest, with_perm):
    n_out = 1 + (N_GROUPS_B if with_perm else 0)
    outs = rest[:n_out]
    y_ref, os1_ref, os2_ref, ls1_ref, ls2_ref = rest[n_out:]
    tm = x_ref.shape[0]
    for o_ref, l_ref, os_ref, ls_ref, d in ((o1_ref, l1_ref, os1_ref, ls1_ref, DILATIONS[1]),
                                            (o2_ref, l2_ref, os2_ref, ls2_ref, DILATIONS[2])):
        for res in range(d):
            ls_ref[pl.ds(res, tm // d, stride=d), :] = l_ref[res]
            for s in range(N_SLABS):
                os_ref[s, pl.ds(res, tm // d, stride=d), :] = (
                    o_ref[res, :, s * LANES:(s + 1) * LANES].astype(F32))
    l0, l1, l2 = l0_ref[...], ls1_ref[...], ls2_ref[...]
    mx = jnp.maximum(jnp.maximum(l0, l1), l2)
    e0, e1, e2 = jnp.exp(l0 - mx), jnp.exp(l1 - mx), jnp.exp(l2 - mx)
    inv = 1.0 / (e0 + e1 + e2)
    w0, w1, w2 = e0 * inv, e1 * inv, e2 * inv
    for h in range(HEADS_B):
        sl = slice(h * HEAD_DIM_B, (h + 1) * HEAD_DIM_B)
        o = (w0[:, h:h + 1] * o0_ref[:, sl].astype(F32) + w1[:, h:h + 1] * os1_ref[h]
             + w2[:, h:h + 1] * os2_ref[h])
        zz = z_ref[:, sl].astype(F32)
        y_ref[:, sl] = (o * (zz * _sigmoid(zz))).astype(BF16)
    r = ALPHA * x_ref[...] + _dot(y_ref[...], w_ref[...])
    _store_stream(_layer_norm_rows(r, g_ref[...], b_ref[...]), outs[0], outs[1:], os1_ref)


def _merge_out_ln(os_, lses, z, w, x, g, b, name, with_perm, tm=512):
    m = x.shape[0]
    n = D_MODEL
    per_seq = SEQ // tm
    tok = lambda width: pl.BlockSpec((tm, width), lambda i: (i, 0))
    res = lambda d, width: pl.BlockSpec((None, d, tm // d, width),
                                        lambda i: (i // per_seq, 0, i % per_seq, 0))
    vec = pl.BlockSpec((1, n), lambda i: (0, 0))
    d1, d2 = DILATIONS[1], DILATIONS[2]
    out_shape, out_specs = _stream_out(tm, with_perm)
    return pl.pallas_call(
        functools.partial(_merge_out_ln_kernel, with_perm=with_perm),
        grid=(m // tm,),
        in_specs=[tok(INNER_B), res(d1, INNER_B), res(d2, INNER_B),
                  tok(LANES), res(d1, LANES), res(d2, LANES),
                  tok(INNER_B),
                  pl.BlockSpec((INNER_B, n), lambda i: (0, 0)),
                  tok(n), vec, vec],
        out_specs=out_specs,
        out_shape=out_shape,
        scratch_shapes=[pltpu.VMEM((tm, INNER_B), BF16),
                        pltpu.VMEM((N_SLABS, tm, LANES), F32), pltpu.VMEM((N_SLABS, tm, LANES), F32),
                        pltpu.VMEM((tm, LANES), F32), pltpu.VMEM((tm, LANES), F32)],
        compiler_params=pltpu.CompilerParams(dimension_semantics=("parallel",),
                                             vmem_limit_bytes=VMEM_LIMIT),
        name=name,
    )(os_[0], os_[1].reshape(BATCH, d1, SEQ // d1, INNER_B), os_[2].reshape(BATCH, d2, SEQ // d2, INNER_B),
      lses[0], lses[1].reshape(BATCH, d1, SEQ // d1, LANES), lses[2].reshape(BATCH, d2, SEQ // d2, LANES),
      z, w, x, g, b)


def _block_diag_lane_blocks(w):
    per = LANES // QKV_BLOCK
    w4 = w.reshape(-1, per, QKV_BLOCK, QKV_BLOCK)
    eye = jnp.eye(per, dtype=w.dtype)
    dense = jnp.einsum('bmij,mp->bmjpi', w4, eye)
    return dense.reshape(-1, LANES, LANES)


def _fold_gate_weights(wq, wk, wv, w_if):
    nblk = INNER_A // QKV_BLOCK
    wif = w_if.reshape(3, nblk, QKV_BLOCK, 2 * HEADS_A)
    hp = lax.Precision.HIGHEST
    wgc = (jnp.einsum('nij,nio->njo', wq, wif[0], precision=hp)
           + jnp.einsum('nij,nio->njo', wk, wif[1], precision=hp)).reshape(INNER_A, 2 * HEADS_A)
    wgm = jnp.einsum('nij,nio->njo', wv, wif[2], precision=hp).reshape(INNER_A, 2 * HEADS_A)
    pad = ((0, 0), (0, LANES - 2 * HEADS_A))
    return jnp.pad(wgc, pad).astype(BF16), jnp.pad(wgm, pad).astype(BF16)


def kernel(x, ln_g, ln_b, a_w_in, a_conv_w, a_conv_b, a_wq, a_wk, a_wv, a_w_if, a_b_if, a_gn_g, a_skip,
           a_w_out, b_w_kv, b_w_in, b_w_out):
    xs = x.reshape(TOKENS, D_MODEL)
    x_in = xs
    for layer in range(N_A_LAYERS):
        bdqk =jnp.concatenate([_block_diag_lane_blocks(a_wq[layer]),
                                _block_diag_lane_blocks(a_wk[layer])], axis=-1).astype(BF16)
        bdv = _block_diag_lane_blocks(a_wv[layer]).astype(BF16)
        wgc, wgm = _fold_gate_weights(a_wq[layer], a_wk[layer], a_wv[layer], a_w_if[layer])
        bg = jnp.pad(a_b_if[layer][None, :], ((0, 0), (0, LANES - 2 * HEADS_A)))
        proj = _proj(x_in, a_w_in, layer, 0, 3 * INNER_A, f"a{layer}_proj")
        q, k, v, xc, wi, cv = _mlstm_pre(proj, a_conv_w[layer], a_conv_b[layer][None, :], bdqk, bdv,
                                         wgc, wgm, bg, f"a{layer}_pre")
        y = _mlstm_core(q, k, v, xc, proj, wi, cv, a_gn_g[layer][None, :], a_skip[layer][None, :],
                        f"a{layer}_mlstm")
        outs = _out_ln(y, a_w_out[layer].astype(BF16), xs, ln_g[layer][None, :], ln_b[layer][None, :],
                       f"a{layer}_out_ln", with_perm=layer == N_A_LAYERS - 1)
        xs = outs[0]
        x_in = xs

    perms = [p.reshape(TOKENS, D_MODEL) for p in outs[1:]]
    kvs = [_proj(perms[g], b_w_kv[None], 0, 2 * g * INNER_B, 2 * INNER_B, f"kv_proj{g}")
           for g in range(N_GROUPS_B)]
    q_scale = HEAD_DIM_B ** -0.5
    for lb in range(N_B_LAYERS):
        layer = N_A_LAYERS + lb
        qs = [_proj(perms[g], b_w_in, lb, g * INNER_B, INNER_B, f"b{lb}_proj{g}", out_scale=q_scale)
              for g in range(N_GROUPS_B)]
        z = _proj(perms[0], b_w_in, lb, N_GROUPS_B * INNER_B, INNER_B, f"b{lb}_projz")
        os_, lses = [], []
        for g in range(N_GROUPS_B):
            o, lse = _attn_group(qs[g], kvs[g], g, f"b{lb}_attn{g}")
            os_.append(o)
            lses.append(lse)
        outs = _merge_out_ln(os_, lses, z, b_w_out[lb].astype(BF16), xs, ln_g[layer][None, :],
                             ln_b[layer][None, :], f"b{lb}_merge_out_ln",
                             with_perm=lb < N_B_LAYERS - 1)
        xs = outs[0]
        perms = [p.reshape(TOKENS, D_MODEL) for p in outs[1:]]
    return xs.reshape(BATCH, SEQ, D_MODEL)
```

```python
import functools

import jax
import jax.numpy as jnp
from jax import lax
from jax.experimental import pallas as pl
from jax.experimental.pallas import tpu as pltpu

D_MODEL = 1024
BATCH = 4
SEQ = 4096
DEPTH = 4
N_A_LAYERS = DEPTH // 2
N_B_LAYERS = DEPTH - N_A_LAYERS
INNER_A = 2 * D_MODEL
HEADS_A = 4
HEAD_DIM_A = INNER_A // HEADS_A
QKV_BLOCK = 4
CONV_K = 4
HEAD_DIM_B = 128
HEADS_B = D_MODEL // HEAD_DIM_B
INNER_B = HEADS_B * HEAD_DIM_B
GROUPS_B = ((128, 1), (512, 4), (2048, 16))
N_GROUPS_B = len(GROUPS_B)
DILATIONS = tuple(d for _, d in GROUPS_B)
BLOCK_B = 128
ALPHA = (2 * DEPTH) ** 0.25
LN_EPS = 1e-5

TOKENS = BATCH * SEQ
LANES = 128
SUBLANES = 8
N_SLABS = D_MODEL // LANES
CHUNK_A = 128
N_LANE_BLOCKS_A = INNER_A // LANES
VMEM_LIMIT = 48 * 1024 * 1024
PROJ_TN = 1024
CV_W_INTER, CV_FLOOR, CV_W_STATE, CV_DECAY = 0, HEADS_A, 2 * HEADS_A, 3 * HEADS_A

F32 = jnp.float32
BF16 = jnp.bfloat16

assert all(w // d == BLOCK_B for w, d in GROUPS_B)


def _dot(a, b):
    return jnp.dot(a, b, preferred_element_type=F32)


def _dot_nt(a, b):
    return lax.dot_general(a, b, (((1,), (1,)), ((), ())), preferred_element_type=F32)


def _sigmoid(x):
    return 1.0 / (1.0 + jnp.exp(-x))


def _layer_norm_rows(r, g, b):
    mu = jnp.mean(r, axis=-1, keepdims=True)
    d = r - mu
    var = jnp.mean(d * d, axis=-1, keepdims=True)
    return d * lax.rsqrt(var + LN_EPS) * g + b


def _store_stream(r, x_ref, perm_refs, slab_ref):
    x_ref[...] = r
    if not perm_refs:
        return
    perm_refs[0][...] = r.astype(BF16)
    tm = r.shape[0]
    for s in range(N_SLABS):
        slab_ref[s] = r[:, s * LANES:(s + 1) * LANES]
    for ref, d in zip(perm_refs[1:], DILATIONS[1:]):
        for res in range(d):
            for s in range(N_SLABS):
                ref[res, :, s * LANES:(s + 1) * LANES] = (
                    slab_ref[s, pl.ds(res, tm // d, stride=d), :].astype(BF16))


def _stream_out(tm, with_perm):
    per_seq = SEQ // tm
    shapes = [jax.ShapeDtypeStruct((TOKENS, D_MODEL), F32)]
    specs = [pl.BlockSpec((tm, D_MODEL), lambda i: (i, 0))]
    if with_perm:
        shapes.append(jax.ShapeDtypeStruct((TOKENS, D_MODEL), BF16))
        specs.append(pl.BlockSpec((tm, D_MODEL), lambda i: (i, 0)))
        for d in DILATIONS[1:]:
            shapes.append(jax.ShapeDtypeStruct((BATCH, d, SEQ // d, D_MODEL), BF16))
            specs.append(pl.BlockSpec((None, d, tm // d, D_MODEL),
                                      lambda i: (i // per_seq, 0, i % per_seq, 0)))
    return shapes, specs


def _proj_kernel(x_ref, w_ref, o_ref, wb_ref, *, out_scale):
    @pl.when(pl.program_id(1) == 0)
    def _():
        wb_ref[...] = w_ref[...].astype(BF16)

    acc = _dot(x_ref[...].astype(BF16), wb_ref[...])
    if out_scale is not None:
        acc = acc * out_scale
    o_ref[...] = acc.astype(o_ref.dtype)


def _proj(x, w, layer, col0, n, name, out_scale=None, tm=2048, tn=PROJ_TN):
    m, k = x.shape
    col_blk0 = col0 // tn
    return pl.pallas_call(
        functools.partial(_proj_kernel, out_scale=out_scale),
        grid=(n // tn, m // tm),
        in_specs=[pl.BlockSpec((tm, k), lambda j, i: (i, 0)),
                  pl.BlockSpec((None, k, tn), lambda j, i: (layer, 0, col_blk0 + j))],
        out_specs=pl.BlockSpec((tm, tn), lambda j, i: (i, j)),
        out_shape=jax.ShapeDtypeStruct((m, n), BF16),
        scratch_shapes=[pltpu.VMEM((k, tn), BF16)],
        compiler_params=pltpu.CompilerParams(dimension_semantics=("parallel", "arbitrary"),
                                             vmem_limit_bytes=VMEM_LIMIT),
        name=name,
    )(x, w)


def _pre_kernel(xm_ref, halo_ref, cw_ref, cb_ref, bdqk_ref, bdv_ref, wgc_ref, wgm_ref, bg_ref,
                q_ref, k_ref, v_ref, xc_ref, wi_ref, cv_ref, ext_ref, m_ref, gates_ref, *, n_tiles):
    tm = xm_ref.shape[0]
    i = pl.program_id(0)
    tile = jnp.minimum(i, n_tiles - 1)
    seq_start = (tile * tm) % SEQ == 0
    prev_seq_start = ((i - 1) * tm) % SEQ == 0

    @pl.when(i == 0)
    def _():
        gates_ref[...] = jnp.zeros_like(gates_ref)
        m_ref[...] = jnp.zeros_like(m_ref)

    m_start = jnp.where(prev_seq_start, 0.0, m_ref[0:1, :])
    m_ref[0:1, :] = _gate_weights(gates_ref[...], m_start, wi_ref, cv_ref)

    halo = jnp.where(seq_start, 0.0, halo_ref[...].astype(F32))
    for blk in range(N_LANE_BLOCKS_A):
        sl = slice(blk * LANES, (blk + 1) * LANES)
        ext_ref[blk, pl.ds(0, SUBLANES, stride=2), :] = halo[:, sl]
        ext_ref[blk, pl.ds(2 * SUBLANES, tm, stride=2), :] = xm_ref[:, sl].astype(F32)
        acc = cb_ref[:, sl]
        for j in range(CONV_K):
            off = 2 * (SUBLANES - (CONV_K - 1) + j)
            acc = acc + cw_ref[j:j + 1, sl] * ext_ref[blk, pl.ds(off, tm, stride=2), :]
        xc_ref[:, sl] = (acc * _sigmoid(acc)).astype(BF16)
        qk = _dot(xc_ref[:, sl], bdqk_ref[blk])
        q_ref[:, sl] = qk[:, :LANES].astype(BF16)
        k_ref[:, sl] = qk[:, LANES:].astype(BF16)
        v_ref[:, sl] = _dot(xm_ref[:, sl], bdv_ref[blk]).astype(BF16)

    gates_ref[...] = _dot(xc_ref[...], wgc_ref[...]) + _dot(xm_ref[...], wgm_ref[...]) + bg_ref[...]


def _gate_weights(gates, m_start, wi_ref, cv_ref):
    tm = gates.shape[0]
    L = CHUNK_A
    scale = HEAD_DIM_A ** -0.5
    log_sig = jnp.minimum(gates, 0.0) - jnp.log(1.0 + jnp.exp(-jnp.abs(gates)))
    log_f = pltpu.roll(log_sig, LANES - HEADS_A, 1)
    row = lax.broadcasted_iota(jnp.int32, (L, L), 0)
    col = lax.broadcasted_iota(jnp.int32, (L, L), 1)
    tri = col <= row
    tri_b = jnp.where(tri, 1.0, 0.0).astype(BF16)
    lane = lax.broadcasted_iota(jnp.int32, (L, LANES), 1)
    head_lane = lax.broadcasted_iota(jnp.int32, (1, LANES), 1) < HEADS_A
    chunks = [slice(c * L, (c + 1) * L) for c in range(tm // L)]

    parts = []
    for rows in chunks:
        li = gates[rows, :]
        lf = log_f[rows, :]
        lf_hi = lf.astype(BF16)
        lf_mid = (lf - lf_hi.astype(F32)).astype(BF16)
        lf_lo = (lf - lf_hi.astype(F32) - lf_mid.astype(F32)).astype(BF16)
        b = _dot(tri_b, lf_hi) + _dot(tri_b, lf_mid) + _dot(tri_b, lf_lo)
        g_tot = b[L - 1:L, :]
        li_rows = li.T[0:SUBLANES, :]
        b_rows = b.T[0:SUBLANES, :]
        row_max = jnp.zeros((L, LANES), F32)
        for h in range(HEADS_A):
            d_intra = jnp.where(tri, b[:, h:h + 1] - b_rows[h:h + 1, :] + li_rows[h:h + 1, :], -jnp.inf)
            wi_ref[h, rows, :] = d_intra
            row_max = jnp.where(lane == h, jnp.max(d_intra, axis=1, keepdims=True), row_max)
        lw = g_tot - b + li
        parts.append((b, g_tot, lw, jnp.max(lw, axis=0, keepdims=True), row_max))

    m_list = [m_start]
    for b, g_tot, lw, lw_max, row_max in parts:
        m_list.append(jnp.where(head_lane, jnp.maximum(g_tot + m_list[-1], lw_max), 0.0))

    for c, rows in enumerate(chunks):
        b, g_tot, lw, lw_max, row_max = parts[c]
        m_prev, m_new = m_list[c], m_list[c + 1]
        a_inter = b + m_prev
        m_t = jnp.maximum(a_inter, row_max)
        for h in range(HEADS_A):
            wi_ref[h, rows, :] = jnp.exp(wi_ref[h, rows, :] - m_t[:, h:h + 1]) * scale
        w_inter = jnp.exp(a_inter - m_t)
        floor = jnp.exp(-m_t)
        w_state = jnp.exp(lw - m_new) * scale
        decay = jnp.broadcast_to(jnp.exp(g_tot + m_prev - m_new), (L, LANES))
        cv_ref[rows, :] = jnp.where(
            lane < CV_FLOOR, w_inter,
            jnp.where(lane < CV_W_STATE, pltpu.roll(floor, CV_FLOOR, 1),
                      jnp.where(lane < CV_DECAY, pltpu.roll(w_state, CV_W_STATE, 1),
                                jnp.where(lane < CV_DECAY + HEADS_A, pltpu.roll(decay, CV_DECAY, 1),
                                          0.0))))
    return m_list[-1]


def _mlstm_pre(proj, cw, cb, bdqk, bdv, wgc, wgm, bg, name, tm=512):
    n_rows = proj.shape[0]
    n_tiles = n_rows // tm
    act = jax.ShapeDtypeStruct((n_rows, INNER_A), BF16)
    cur = lambda i: jnp.minimum(i, n_tiles - 1)
    prev = lambda i: jnp.maximum(i - 1, 0)
    row_spec = pl.BlockSpec((tm, INNER_A), lambda i: (cur(i), 0))
    full = lambda shape: pl.BlockSpec(shape, lambda i: (0,) * len(shape))
    return pl.pallas_call(
        functools.partial(_pre_kernel, n_tiles=n_tiles),
        grid=(n_tiles + 1,),
        in_specs=[row_spec,
                  pl.BlockSpec((SUBLANES, INNER_A),
                               lambda i: (jnp.maximum(cur(i) * (tm // SUBLANES) - 1, 0), 0)),
                  full((CONV_K, INNER_A)), full((1, INNER_A)),
                  full((N_LANE_BLOCKS_A, LANES, 2 * LANES)), full((N_LANE_BLOCKS_A, LANES, LANES)),
                  full((INNER_A, LANES)), full((INNER_A, LANES)), full((1, LANES))],
        out_specs=[row_spec, row_spec, row_spec, row_spec,
                   pl.BlockSpec((HEADS_A, tm, CHUNK_A), lambda i: (0, prev(i), 0)),
                   pl.BlockSpec((tm, LANES), lambda i: (prev(i), 0))],
        out_shape=[act, act, act, act,
                   jax.ShapeDtypeStruct((HEADS_A, n_rows, CHUNK_A), F32),
                   jax.ShapeDtypeStruct((n_rows, LANES), F32)],
        scratch_shapes=[pltpu.VMEM((N_LANE_BLOCKS_A, 2 * (tm + SUBLANES), LANES), F32),
                        pltpu.VMEM((SUBLANES, LANES), F32), pltpu.VMEM((tm, LANES), F32)],
        compiler_params=pltpu.CompilerParams(dimension_semantics=("arbitrary",),
                                             vmem_limit_bytes=VMEM_LIMIT),
        name=name,
    )(proj, proj, cw, cb, bdqk, bdv, wgc, wgm, bg)


def _mlstm_kernel(q_ref, k_ref, v_ref, xc_ref, z_ref, o_ref, wi_ref, cv_ref, gn_ref, sk_ref, y_ref,
                  ct_ref, n_ref):
    @pl.when(pl.program_id(1) == 0)
    def _():
        ct_ref[...] = jnp.zeros_like(ct_ref)
        n_ref[...] = jnp.zeros_like(n_ref)

    for c in range(q_ref.shape[0] // CHUNK_A):
        r = slice(c * CHUNK_A, (c + 1) * CHUNK_A)
        for h in range(HEADS_A):
            hs = slice(h * HEAD_DIM_A, (h + 1) * HEAD_DIM_A)
            qb = q_ref[r, hs]
            kb = k_ref[r, hs]
            vb = v_ref[r, hs]
            w_inter = cv_ref[r, CV_W_INTER + h:CV_W_INTER + h + 1]
            floor = cv_ref[r, CV_FLOOR + h:CV_FLOOR + h + 1]
            w_state = cv_ref[r, CV_W_STATE + h:CV_W_STATE + h + 1]
            decay = cv_ref[c * CHUNK_A:c * CHUNK_A + 1, CV_DECAY + h:CV_DECAY + h + 1]

            sc = _dot_nt(qb, kb) * wi_ref[h, r, :]
            ct = ct_ref[h]
            num = w_inter * _dot(qb, ct.astype(BF16)) + _dot(sc.astype(BF16), vb)
            den = (w_inter * jnp.sum(qb.astype(F32) * n_ref[h], axis=1, keepdims=True)
                   + jnp.sum(sc, axis=1, keepdims=True))
            h_tilde = num * (1.0 / jnp.maximum(jnp.abs(den), floor))

            kw = kb.astype(F32) * w_state
            ct_ref[h] = decay * ct + _dot(kw.T.astype(BF16), vb)
            n_ref[h] = decay * n_ref[h] + jnp.sum(kw, axis=0, keepdims=True)

            hg = h_tilde * _sigmoid(o_ref[r, hs].astype(F32))
            mu = jnp.mean(hg, axis=1, keepdims=True)
            dlt = hg - mu
            var = jnp.mean(dlt * dlt, axis=1, keepdims=True)
            hn = dlt * lax.rsqrt(var + LN_EPS) * gn_ref[:, hs]
            zz = z_ref[r, hs].astype(F32)
            y = (hn + sk_ref[:, hs] * xc_ref[r, hs].astype(F32)) * (zz * _sigmoid(zz))
            y_ref[r, hs] = y.astype(BF16)


def _mlstm_core(q, k, v, xc, proj, wi, cv, gn, skip, name, chunks_per_step=2):
    L = chunks_per_step * CHUNK_A
    dh = HEAD_DIM_A
    steps = SEQ // L
    row = lambda b, c: b * steps + c
    act_spec = pl.BlockSpec((L, INNER_A), lambda b, c: (row(b, c), 0))
    vec_spec = pl.BlockSpec((1, INNER_A), lambda b, c: (0, 0))
    return pl.pallas_call(
        _mlstm_kernel,
        grid=(BATCH, steps),
        in_specs=[act_spec, act_spec, act_spec, act_spec,
                  pl.BlockSpec((L, INNER_A), lambda b, c: (row(b, c), 1)),
                  pl.BlockSpec((L, INNER_A), lambda b, c: (row(b, c), 2)),
                  pl.BlockSpec((HEADS_A, L, CHUNK_A), lambda b, c: (0, row(b, c), 0)),
                  pl.BlockSpec((L, LANES), lambda b, c: (row(b, c), 0)),
                  vec_spec, vec_spec],
        out_specs=act_spec,
        out_shape=jax.ShapeDtypeStruct((TOKENS, INNER_A), BF16),
        scratch_shapes=[pltpu.VMEM((HEADS_A, dh, dh), F32), pltpu.VMEM((HEADS_A, 1, dh), F32)],
        compiler_params=pltpu.CompilerParams(
            dimension_semantics=("parallel", "arbitrary"), vmem_limit_bytes=VMEM_LIMIT),
        name=name,
    )(q, k, v, xc, proj, proj, wi, cv, gn, skip)


def _out_ln_kernel(y_ref, w_ref, x_ref, g_ref, b_ref, *rest, with_perm):
    n_out = 1 + (N_GROUPS_B if with_perm else 0)
    outs, scratch = rest[:n_out], rest[n_out:]
    r = ALPHA * x_ref[...] + _dot(y_ref[...], w_ref[...])
    _store_stream(_layer_norm_rows(r, g_ref[...], b_ref[...]), outs[0], outs[1:],
                  scratch[0] if with_perm else None)


def _out_ln(y, w, x, g, b, name, with_perm, tm=512):
    m, k = y.shape
    n = w.shape[1]
    out_shape, out_specs = _stream_out(tm, with_perm)
    return pl.pallas_call(
        functools.partial(_out_ln_kernel, with_perm=with_perm),
        grid=(m // tm,),
        in_specs=[pl.BlockSpec((tm, k), lambda i: (i, 0)),
                  pl.BlockSpec((k, n), lambda i: (0, 0)),
                  pl.BlockSpec((tm, n), lambda i: (i, 0)),
                  pl.BlockSpec((1, n), lambda i: (0, 0)),
                  pl.BlockSpec((1, n), lambda i: (0, 0))],
        out_specs=out_specs,
        out_shape=out_shape,
        scratch_shapes=[pltpu.VMEM((N_SLABS, tm, LANES), F32)] if with_perm else [],
        compiler_params=pltpu.CompilerParams(dimension_semantics=("parallel",),
                                             vmem_limit_bytes=VMEM_LIMIT),
        name=name,
    )(y, w, x, g, b)


def _attn_kernel(q_ref, kvp_ref, kvc_ref, o_ref, lse_ref, *, dilation, blocks_per_seq):
    n = BLOCK_B
    n_blocks = q_ref.shape[0] // n
    qi = lax.broadcasted_iota(jnp.int32, (n, n), 0)
    kj = lax.broadcasted_iota(jnp.int32, (n, n), 1)
    lower = kj <= qi
    diag = kj == qi
    dist = jnp.bitwise_and(qi - kj, n - 1).astype(F32)
    heads = range(HEADS_B)
    ksl = [slice(h * HEAD_DIM_B, (h + 1) * HEAD_DIM_B) for h in heads]
    vsl = [slice(INNER_B + h * HEAD_DIM_B, INNER_B + (h + 1) * HEAD_DIM_B) for h in heads]
    slopes = [2.0 ** (-8.0 * (h + 1.0) / HEADS_B) * dilation for h in heads]

    for blk in range(n_blocks):
        rows = slice(blk * n, (blk + 1) * n)
        prev_ref, prev_rows = ((kvp_ref, slice(0, n)) if blk == 0
                               else (kvc_ref, slice((blk - 1) * n, blk * n)))
        first = (pl.program_id(0) * n_blocks + blk) % blocks_per_seq == 0
        prev_bias = jnp.where(first, -jnp.inf, 0.0)

        both_rows = slice((blk - 1) * n, (blk + 1) * n)
        scores = []
        for h in heads:
            qh = q_ref[rows, ksl[h]]
            if blk == 0:
                s_prev = _dot_nt(qh, prev_ref[prev_rows, ksl[h]])
                s_cur = _dot_nt(qh, kvc_ref[rows, ksl[h]])
            else:
                s_both = _dot_nt(qh, kvc_ref[both_rows, ksl[h]])
                s_prev, s_cur = s_both[:, :n], s_both[:, n:]
            s_prev = s_prev + prev_bias
            s = jnp.where(lower, s_cur, s_prev) - slopes[h] * dist
            s_diag = (jnp.max(jnp.where(diag, s_prev, -jnp.inf), axis=-1, keepdims=True)
                      - slopes[h] * float(n))
            m = jnp.maximum(jnp.max(s, axis=-1, keepdims=True), s_diag)
            scores.append((s, s_diag, m))
        probs = []
        for h in heads:
            s, s_diag, m = scores[h]
            e = jnp.exp(s - m)
            e_diag = jnp.exp(s_diag - m)
            den = jnp.sum(e, axis=-1, keepdims=True) + e_diag
            probs.append((jnp.where(lower, e, 0.0).astype(BF16), jnp.where(lower, 0.0, e).astype(BF16),
                          e_diag, den))
        lse_all = jnp.zeros((n, LANES), F32)
        for h in heads:
            p_cur, p_prev, e_diag, den = probs[h]
            vp = prev_ref[prev_rows, vsl[h]]
            if blk == 0:
                o = _dot(p_cur, kvc_ref[rows, vsl[h]]) + _dot(p_prev, vp)
            else:
                o = _dot(jnp.concatenate([p_prev, p_cur], axis=1), kvc_ref[both_rows, vsl[h]])
            o = o + e_diag * vp.astype(F32)
            o_ref[rows, ksl[h]] = (o * (1.0 / den)).astype(BF16)
            lse_all = jnp.where(kj == h, scores[h][2] + jnp.log(den), lse_all)
        lse_ref[rows, :] = lse_all


def _attn_group(q, kv, group, name, blocks_per_step=8):
    dilation = DILATIONS[group]
    blocks_per_seq = SEQ // dilation // BLOCK_B
    tq = blocks_per_step * BLOCK_B
    return pl.pallas_call(
        functools.partial(_attn_kernel, dilation=float(dilation), blocks_per_seq=blocks_per_seq),
        grid=(TOKENS // tq,),
        in_specs=[pl.BlockSpec((tq, INNER_B), lambda n: (n, 0)),
                  pl.BlockSpec((BLOCK_B, 2 * INNER_B),
                               lambda n: (jnp.maximum(n * blocks_per_step - 1, 0), 0)),
                  pl.BlockSpec((tq, 2 * INNER_B), lambda n: (n, 0))],
        out_specs=[pl.BlockSpec((tq, INNER_B), lambda n: (n, 0)),
                   pl.BlockSpec((tq, LANES), lambda n: (n, 0))],
        out_shape=[jax.ShapeDtypeStruct((TOKENS, INNER_B), BF16),
                   jax.ShapeDtypeStruct((TOKENS, LANES), F32)],
        compiler_params=pltpu.CompilerParams(dimension_semantics=("arbitrary",),
                                             vmem_limit_bytes=VMEM_LIMIT),
        name=name,
    )(q, kv, kv)


def _merge_out_ln_kernel(o0_ref, o1_ref, o2_ref, l0_ref, l1_ref, l2_ref, z_ref, w_ref, x_ref,
                         g_ref, b_ref, *rest, with_perm):
    n_out = 1 + (N_GROUPS_B if with_perm else 0)
    outs = rest[:n_out]
    y_ref, os1_ref, os2_ref, ls1_ref, ls2_ref = rest[n_out:]
    tm = x_ref.shape[0]
    for o_ref, l_ref, os_ref, ls_ref, d in ((o1_ref, l1_ref, os1_ref, ls1_ref, DILATIONS[1]),
                                            (o2_ref, l2_ref, os2_ref, ls2_ref, DILATIONS[2])):
        for res in range(d):
            ls_ref[pl.ds(res, tm // d, stride=d), :] = l_ref[res]
            for s in range(N_SLABS):
                os_ref[s, pl.ds(res, tm // d, stride=d), :] = (
                    o_ref[res, :, s * LANES:(s + 1) * LANES].astype(F32))
    l0, l1, l2 = l0_ref[...], ls1_ref[...], ls2_ref[...]
    mx = jnp.maximum(jnp.maximum(l0, l1), l2)
    e0, e1, e2 = jnp.exp(l0 - mx), jnp.exp(l1 - mx), jnp.exp(l2 - mx)
    inv = 1.0 / (e0 + e1 + e2)
    w0, w1, w2 = e0 * inv, e1 * inv, e2 * inv
    for h in range(HEADS_B):
        sl = slice(h * HEAD_DIM_B, (h + 1) * HEAD_DIM_B)
        o = (w0[:, h:h + 1] * o0_ref[:, sl].astype(F32) + w1[:, h:h + 1] * os1_ref[h]
             + w2[:, h:h + 1] * os2_ref[h])
        zz = z_ref[:, sl].astype(F32)
        y_ref[:, sl] = (o * (zz * _sigmoid(zz))).astype(BF16)
    r = ALPHA * x_ref[...] + _dot(y_ref[...], w_ref[...])
    _store_stream(_layer_norm_rows(r, g_ref[...], b_ref[...]), outs[0], outs[1:], os1_ref)


def _merge_out_ln(os_, lses, z, w, x, g, b, name, with_perm, tm=512):
    m = x.shape[0]
    n = D_MODEL
    per_seq = SEQ // tm
    tok = lambda width: pl.BlockSpec((tm, width), lambda i: (i, 0))
    res = lambda d, width: pl.BlockSpec((None, d, tm // d, width),
                                        lambda i: (i // per_seq, 0, i % per_seq, 0))
    vec = pl.BlockSpec((1, n), lambda i: (0, 0))
    d1, d2 = DILATIONS[1], DILATIONS[2]
    out_shape, out_specs = _stream_out(tm, with_perm)
    return pl.pallas_call(
        functools.partial(_merge_out_ln_kernel, with_perm=with_perm),
        grid=(m // tm,),
        in_specs=[tok(INNER_B), res(d1, INNER_B), res(d2, INNER_B),
                  tok(LANES), res(d1, LANES), res(d2, LANES),
                  tok(INNER_B),
                  pl.BlockSpec((INNER_B, n), lambda i: (0, 0)),
                  tok(n), vec, vec],
        out_specs=out_specs,
        out_shape=out_shape,
        scratch_shapes=[pltpu.VMEM((tm, INNER_B), BF16),
                        pltpu.VMEM((N_SLABS, tm, LANES), F32), pltpu.VMEM((N_SLABS, tm, LANES), F32),
                        pltpu.VMEM((tm, LANES), F32), pltpu.VMEM((tm, LANES), F32)],
        compiler_params=pltpu.CompilerParams(dimension_semantics=("parallel",),
                                             vmem_limit_bytes=VMEM_LIMIT),
        name=name,
    )(os_[0], os_[1].reshape(BATCH, d1, SEQ // d1, INNER_B), os_[2].reshape(BATCH, d2, SEQ // d2, INNER_B),
      lses[0], lses[1].reshape(BATCH, d1, SEQ // d1, LANES), lses[2].reshape(BATCH, d2, SEQ // d2, LANES),
      z, w, x, g, b)


def _block_diag_lane_blocks(w):
    per = LANES // QKV_BLOCK
    w4 = w.reshape(-1, per, QKV_BLOCK, QKV_BLOCK)
    eye = jnp.eye(per, dtype=w.dtype)
    dense = jnp.einsum('bmij,mp->bmjpi', w4, eye)
    return dense.reshape(-1, LANES, LANES)


def _fold_gate_weights(wq, wk, wv, w_if):
    nblk = INNER_A // QKV_BLOCK
    wif = w_if.reshape(3, nblk, QKV_BLOCK, 2 * HEADS_A)
    hp = lax.Precision.HIGHEST
    wgc = (jnp.einsum('nij,nio->njo', wq, wif[0], precision=hp)
           + jnp.einsum('nij,nio->njo', wk, wif[1], precision=hp)).reshape(INNER_A, 2 * HEADS_A)
    wgm = jnp.einsum('nij,nio->njo', wv, wif[2], precision=hp).reshape(INNER_A, 2 * HEADS_A)
    pad = ((0, 0), (0, LANES - 2 * HEADS_A))
    return jnp.pad(wgc, pad).astype(BF16), jnp.pad(wgm, pad).astype(BF16)


def kernel(x, ln_g, ln_b, a_w_in, a_conv_w, a_conv_b, a_wq, a_wk, a_wv, a_w_if, a_b_if, a_gn_g, a_skip,
           a_w_out, b_w_kv, b_w_in, b_w_out):
    xs = x.reshape(TOKENS, D_MODEL)
    x_in = xs
    for layer in range(N_A_LAYERS):
        bdqk =jnp.concatenate([_block_diag_lane_blocks(a_wq[layer]),
                                _block_diag_lane_blocks(a_wk[layer])], axis=-1).astype(BF16)
        bdv = _block_diag_lane_blocks(a_wv[layer]).astype(BF16)
        wgc, wgm = _fold_gate_weights(a_wq[layer], a_wk[layer], a_wv[layer], a_w_if[layer])
        bg = jnp.pad(a_b_if[layer][None, :], ((0, 0), (0, LANES - 2 * HEADS_A)))
        proj = _proj(x_in, a_w_in, layer, 0, 3 * INNER_A, f"a{layer}_proj")
        q, k, v, xc, wi, cv = _mlstm_pre(proj, a_conv_w[layer], a_conv_b[layer][None, :], bdqk, bdv,
                                         wgc, wgm, bg, f"a{layer}_pre")
        y = _mlstm_core(q, k, v, xc, proj, wi, cv, a_gn_g[layer][None, :], a_skip[layer][None, :],
                        f"a{layer}_mlstm")
        outs = _out_ln(y, a_w_out[layer].astype(BF16), xs, ln_g[layer][None, :], ln_b[layer][None, :],
                       f"a{layer}_out_ln", with_perm=layer == N_A_LAYERS - 1)
        xs = outs[0]
        x_in = xs

    perms = [p.reshape(TOKENS, D_MODEL) for p in outs[1:]]
    kvs = [_proj(perms[g], b_w_kv[None], 0, 2 * g * INNER_B, 2 * INNER_B, f"kv_proj{g}")
           for g in range(N_GROUPS_B)]
    q_scale = HEAD_DIM_B ** -0.5
    for lb in range(N_B_LAYERS):
        layer = N_A_LAYERS + lb
        qs = [_proj(perms[g], b_w_in, lb, g * INNER_B, INNER_B, f"b{lb}_proj{g}", out_scale=q_scale)
              for g in range(N_GROUPS_B)]
        z = _proj(perms[0], b_w_in, lb, N_GROUPS_B * INNER_B, INNER_B, f"b{lb}_projz")
        os_, lses = [], []
        for g in range(N_GROUPS_B):
            o, lse = _attn_group(qs[g], kvs[g], g, f"b{lb}_attn{g}")
            os_.append(o)
            lses.append(lse)
        outs = _merge_out_ln(os_, lses, z, b_w_out[lb].astype(BF16), xs, ln_g[layer][None, :],
                             ln_b[layer][None, :], f"b{lb}_merge_out_ln",
                             with_perm=lb < N_B_LAYERS - 1)
        xs = outs[0]
        perms = [p.reshape(TOKENS, D_MODEL) for p in outs[1:]]
    return xs.reshape(BATCH, SEQ, D_MODEL)
```

```python
import functools

import jax
import jax.numpy as jnp
from jax import lax
from jax.experimental import pallas as pl
from jax.experimental.pallas import tpu as pltpu

D_MODEL = 1024
BATCH = 4
SEQ = 4096
DEPTH = 4
N_A_LAYERS = DEPTH // 2
N_B_LAYERS = DEPTH - N_A_LAYERS
INNER_A = 2 * D_MODEL
HEADS_A = 4
HEAD_DIM_A = INNER_A // HEADS_A
QKV_BLOCK = 4
CONV_K = 4
HEAD_DIM_B = 128
HEADS_B = D_MODEL // HEAD_DIM_B
INNER_B = HEADS_B * HEAD_DIM_B
GROUPS_B = ((128, 1), (512, 4), (2048, 16))
N_GROUPS_B = len(GROUPS_B)
DILATIONS = tuple(d for _, d in GROUPS_B)
BLOCK_B = 128
RES_STEP = 4
assert DILATIONS == (1, RES_STEP, RES_STEP * RES_STEP)
ALPHA = (2 * DEPTH) ** 0.25
LN_EPS = 1e-5

TOKENS = BATCH * SEQ
LANES = 128
SUBLANES = 8
N_SLABS = D_MODEL // LANES
CHUNK_A = 256
N_LANE_BLOCKS_A = INNER_A // LANES
VMEM_LIMIT = 48 * 1024 * 1024
PROJ_TN = 1024
CV_W_INTER, CV_FLOOR, CV_W_STATE, CV_DECAY = 0, HEADS_A, 2 * HEADS_A, 3 * HEADS_A

F32 = jnp.float32
BF16 = jnp.bfloat16

assert all(w // d == BLOCK_B for w, d in GROUPS_B)


def _dot(a, b):
    return jnp.dot(a, b, preferred_element_type=F32)


def _dot_nt(a, b):
    return lax.dot_general(a, b, (((1,), (1,)), ((), ())), preferred_element_type=F32)


def _sigmoid(x):
    return 1.0 / (1.0 + jnp.exp(-x))


def _layer_norm_rows(r, g, b):
    mu = jnp.mean(r, axis=-1, keepdims=True)
    d = r - mu
    var = jnp.mean(d * d, axis=-1, keepdims=True)
    return d * lax.rsqrt(var + LN_EPS) * g + b


def _store_stream(r, x_ref, perm_refs, slab_ref, slab4_ref):
    x_ref[...] = r
    if not perm_refs:
        return
    xb1_ref, xb4_ref, xb16_ref = perm_refs
    xb1_ref[...] = r.astype(BF16)
    tm = r.shape[0]
    q4, q16 = tm // RES_STEP, tm // (RES_STEP * RES_STEP)
    for s in range(N_SLABS):
        lanes = slice(s * LANES, (s + 1) * LANES)
        slab_ref[s] = r[:, lanes]
        for r4 in range(RES_STEP):
            part = slab_ref[s, pl.ds(r4, q4, stride=RES_STEP), :]
            slab4_ref[s, r4 * q4:(r4 + 1) * q4, :] = part
            xb4_ref[r4, :, lanes] = part.astype(BF16)
        for r4 in range(RES_STEP):
            for a in range(RES_STEP):
                xb16_ref[r4 + RES_STEP * a, :, lanes] = (
                    slab4_ref[s, pl.ds(r4 * q4 + a, q16, stride=RES_STEP), :].astype(BF16))


def _stream_out(tm, with_perm):
    per_seq = SEQ // tm
    shapes = [jax.ShapeDtypeStruct((TOKENS, D_MODEL), F32)]
    specs = [pl.BlockSpec((tm, D_MODEL), lambda i: (i, 0))]
    if with_perm:
        shapes.append(jax.ShapeDtypeStruct((TOKENS, D_MODEL), BF16))
        specs.append(pl.BlockSpec((tm, D_MODEL), lambda i: (i, 0)))
        for d in DILATIONS[1:]:
            shapes.append(jax.ShapeDtypeStruct((BATCH, d, SEQ // d, D_MODEL), BF16))
            specs.append(pl.BlockSpec((None, d, tm // d, D_MODEL),
                                      lambda i: (i // per_seq, 0, i % per_seq, 0)))
    return shapes, specs


def _proj_kernel(x_ref, w_ref, o_ref, wb_ref, *, out_scale):
    @pl.when(pl.program_id(1) == 0)
    def _():
        wb_ref[...] = w_ref[...].astype(BF16)

    acc = _dot(x_ref[...].astype(BF16), wb_ref[...])
    if out_scale is not None:
        acc = acc * out_scale
    o_ref[...] = acc.astype(o_ref.dtype)


def _proj(x, w, layer, col0, n, name, out_scale=None, tm=2048, tn=PROJ_TN):
    m, k = x.shape
    col_blk0 = col0 // tn
    return pl.pallas_call(
        functools.partial(_proj_kernel, out_scale=out_scale),
        grid=(n // tn, m // tm),
        in_specs=[pl.BlockSpec((tm, k), lambda j, i: (i, 0)),
                  pl.BlockSpec((None, k, tn), lambda j, i: (layer, 0, col_blk0 + j))],
        out_specs=pl.BlockSpec((tm, tn), lambda j, i: (i, j)),
        out_shape=jax.ShapeDtypeStruct((m, n), BF16),
        scratch_shapes=[pltpu.VMEM((k, tn), BF16)],
        compiler_params=pltpu.CompilerParams(dimension_semantics=("parallel", "arbitrary"),
                                             vmem_limit_bytes=VMEM_LIMIT),
        name=name,
    )(x, w)


def _pre_kernel(xm_ref, halo_ref, cw_ref, cb_ref, bdqk_ref, bdv_ref, wgc_ref, wgm_ref, bg_ref,
                q_ref, k_ref, v_ref, xc_ref, wi_ref, cv_ref, ext_ref, m_ref, gates_ref, *, n_tiles):
    tm = xm_ref.shape[0]
    i = pl.program_id(0)
    tile = jnp.minimum(i, n_tiles - 1)
    seq_start = (tile * tm) % SEQ == 0
    prev_seq_start = ((i - 1) * tm) % SEQ == 0

    @pl.when(i == 0)
    def _():
        gates_ref[...] = jnp.zeros_like(gates_ref)
        m_ref[...] = jnp.zeros_like(m_ref)

    m_start = jnp.where(prev_seq_start, 0.0, m_ref[0:1, :])
    m_ref[0:1, :] = _gate_weights(gates_ref[...], m_start, wi_ref, cv_ref)

    halo = jnp.where(seq_start, 0.0, halo_ref[...].astype(F32))
    for blk in range(N_LANE_BLOCKS_A):
        sl = slice(blk * LANES, (blk + 1) * LANES)
        ext_ref[blk, pl.ds(0, SUBLANES, stride=2), :] = halo[:, sl]
        ext_ref[blk, pl.ds(2 * SUBLANES, tm, stride=2), :] = xm_ref[:, sl].astype(F32)
        acc = cb_ref[:, sl]
        for j in range(CONV_K):
            off = 2 * (SUBLANES - (CONV_K - 1) + j)
            acc = acc + cw_ref[j:j + 1, sl] * ext_ref[blk, pl.ds(off, tm, stride=2), :]
        xc_ref[:, sl] = (acc * _sigmoid(acc)).astype(BF16)
        qk = _dot(xc_ref[:, sl], bdqk_ref[blk])
        q_ref[:, sl] = qk[:, :LANES].astype(BF16)
        k_ref[:, sl] = qk[:, LANES:].astype(BF16)
        v_ref[:, sl] = _dot(xm_ref[:, sl], bdv_ref[blk]).astype(BF16)

    gates_ref[...] = _dot(xc_ref[...], wgc_ref[...]) + _dot(xm_ref[...], wgm_ref[...]) + bg_ref[...]


def _gate_weights(gates, m_start, wi_ref, cv_ref):
    tm = gates.shape[0]
    L = CHUNK_A
    scale = HEAD_DIM_A ** -0.5
    log_sig = jnp.minimum(gates, 0.0) - jnp.log(1.0 + jnp.exp(-jnp.abs(gates)))
    log_f = pltpu.roll(log_sig, LANES - HEADS_A, 1)
    row = lax.broadcasted_iota(jnp.int32, (L, L), 0)
    col = lax.broadcasted_iota(jnp.int32, (L, L), 1)
    tri = col <= row
    tri_b = jnp.where(tri, 1.0, 0.0).astype(BF16)
    lane = lax.broadcasted_iota(jnp.int32, (L, LANES), 1)
    head_lane = lax.broadcasted_iota(jnp.int32, (1, LANES), 1) < HEADS_A
    chunks = [slice(c * L, (c + 1) * L) for c in range(tm // L)]

    parts = []
    for rows in chunks:
        li = gates[rows, :]
        lf = log_f[rows, :]
        lf_hi = lf.astype(BF16)
        lf_mid = (lf - lf_hi.astype(F32)).astype(BF16)
        lf_lo = (lf - lf_hi.astype(F32) - lf_mid.astype(F32)).astype(BF16)
        b = _dot(tri_b, lf_hi) + _dot(tri_b, lf_mid) + _dot(tri_b, lf_lo)
        g_tot = b[L - 1:L, :]
        li_rows = li.T[0:SUBLANES, :]
        b_rows = b.T[0:SUBLANES, :]
        row_max = jnp.zeros((L, LANES), F32)
        for h in range(HEADS_A):
            d_intra = jnp.where(tri, b[:, h:h + 1] - b_rows[h:h + 1, :] + li_rows[h:h + 1, :], -jnp.inf)
            wi_ref[h, rows, :] = d_intra
            row_max = jnp.where(lane == h, jnp.max(d_intra, axis=1, keepdims=True), row_max)
        lw = g_tot - b + li
        parts.append((b, g_tot, lw, jnp.max(lw, axis=0, keepdims=True), row_max))

    m_list = [m_start]
    for b, g_tot, lw, lw_max, row_max in parts:
        m_list.append(jnp.where(head_lane, jnp.maximum(g_tot + m_list[-1], lw_max), 0.0))

    for c, rows in enumerate(chunks):
        b, g_tot, lw, lw_max, row_max = parts[c]
        m_prev, m_new = m_list[c], m_list[c + 1]
        a_inter = b + m_prev
        m_t = jnp.maximum(a_inter, row_max)
        for h in range(HEADS_A):
            wi_ref[h, rows, :] = jnp.exp(wi_ref[h, rows, :] - m_t[:, h:h + 1]) * scale
        w_inter = jnp.exp(a_inter - m_t)
        floor = jnp.exp(-m_t)
        w_state = jnp.exp(lw - m_new) * scale
        decay = jnp.broadcast_to(jnp.exp(g_tot + m_prev - m_new), (L, LANES))
        cv_ref[rows, :] = jnp.where(
            lane < CV_FLOOR, w_inter,
            jnp.where(lane < CV_W_STATE, pltpu.roll(floor, CV_FLOOR, 1),
                      jnp.where(lane < CV_DECAY, pltpu.roll(w_state, CV_W_STATE, 1),
                                jnp.where(lane < CV_DECAY + HEADS_A, pltpu.roll(decay, CV_DECAY, 1),
                                          0.0))))
    return m_list[-1]


def _mlstm_pre(proj, cw, cb, bdqk, bdv, wgc, wgm, bg, name, tm=512):
    n_rows = proj.shape[0]
    n_tiles = n_rows // tm
    act = jax.ShapeDtypeStruct((n_rows, INNER_A), BF16)
    cur = lambda i: jnp.minimum(i, n_tiles - 1)
    prev = lambda i: jnp.maximum(i - 1, 0)
    row_spec = pl.BlockSpec((tm, INNER_A), lambda i: (cur(i), 0))
    full = lambda shape: pl.BlockSpec(shape, lambda i: (0,) * len(shape))
    return pl.pallas_call(
        functools.partial(_pre_kernel, n_tiles=n_tiles),
        grid=(n_tiles + 1,),
        in_specs=[row_spec,
                  pl.BlockSpec((SUBLANES, INNER_A),
                               lambda i: (jnp.maximum(cur(i) * (tm // SUBLANES) - 1, 0), 0)),
                  full((CONV_K, INNER_A)), full((1, INNER_A)),
                  full((N_LANE_BLOCKS_A, LANES, 2 * LANES)), full((N_LANE_BLOCKS_A, LANES, LANES)),
                  full((INNER_A, LANES)), full((INNER_A, LANES)), full((1, LANES))],
        out_specs=[row_spec, row_spec, row_spec, row_spec,
                   pl.BlockSpec((HEADS_A, tm, CHUNK_A), lambda i: (0, prev(i), 0)),
                   pl.BlockSpec((tm, LANES), lambda i: (prev(i), 0))],
        out_shape=[act, act, act, act,
                   jax.ShapeDtypeStruct((HEADS_A, n_rows, CHUNK_A), F32),
                   jax.ShapeDtypeStruct((n_rows, LANES), F32)],
        scratch_shapes=[pltpu.VMEM((N_LANE_BLOCKS_A, 2 * (tm + SUBLANES), LANES), F32),
                        pltpu.VMEM((SUBLANES, LANES), F32), pltpu.VMEM((tm, LANES), F32)],
        compiler_params=pltpu.CompilerParams(dimension_semantics=("arbitrary",),
                                             vmem_limit_bytes=VMEM_LIMIT),
        name=name,
    )(proj, proj, cw, cb, bdqk, bdv, wgc, wgm, bg)


def _mlstm_kernel(q_ref, k_ref, v_ref, xc_ref, z_ref, o_ref, wi_ref, cv_ref, gn_ref, sk_ref, y_ref,
                  ct_ref, n_ref):
    @pl.when(pl.program_id(1) == 0)
    def _():
        ct_ref[...] = jnp.zeros_like(ct_ref)
        n_ref[...] = jnp.zeros_like(n_ref)

    for c in range(q_ref.shape[0] // CHUNK_A):
        r = slice(c * CHUNK_A, (c + 1) * CHUNK_A)
        for h in range(HEADS_A):
            hs = slice(h * HEAD_DIM_A, (h + 1) * HEAD_DIM_A)
            qb = q_ref[r, hs]
            kb = k_ref[r, hs]
            vb = v_ref[r, hs]
            w_inter = cv_ref[r, CV_W_INTER + h:CV_W_INTER + h + 1]
            floor = cv_ref[r, CV_FLOOR + h:CV_FLOOR + h + 1]
            w_state = cv_ref[r, CV_W_STATE + h:CV_W_STATE + h + 1]
            decay = cv_ref[c * CHUNK_A:c * CHUNK_A + 1, CV_DECAY + h:CV_DECAY + h + 1]

            sc = _dot_nt(qb, kb) * wi_ref[h, r, :]
            num = w_inter * _dot(qb, ct_ref[h].astype(BF16)) + _dot(sc.astype(BF16), vb)
            den = (w_inter * jnp.sum(qb.astype(F32) * n_ref[h], axis=1, keepdims=True)
                   + jnp.sum(sc, axis=1, keepdims=True))
            h_tilde = num * (1.0 / jnp.maximum(jnp.abs(den), floor))

            kw = kb.astype(F32) * w_state
            ct_ref[h] = decay * ct_ref[h] + _dot(kw.T.astype(BF16), vb)
            n_ref[h] = decay * n_ref[h] + jnp.sum(kw, axis=0, keepdims=True)

            hg = h_tilde * _sigmoid(o_ref[r, hs].astype(F32))
            mu = jnp.mean(hg, axis=1, keepdims=True)
            dlt = hg - mu
            var = jnp.mean(dlt * dlt, axis=1, keepdims=True)
            hn = dlt * lax.rsqrt(var + LN_EPS) * gn_ref[:, hs]
            zz = z_ref[r, hs].astype(F32)
            y = (hn + sk_ref[:, hs] * xc_ref[r, hs].astype(F32)) * (zz * _sigmoid(zz))
            y_ref[r, hs] = y.astype(BF16)


def _mlstm_core(q, k, v, xc, proj, wi, cv, gn, skip, name, chunks_per_step=2):
    L = chunks_per_step * CHUNK_A
    dh = HEAD_DIM_A
    steps = SEQ // L
    row = lambda b, c: b * steps + c
    act_spec = pl.BlockSpec((L, INNER_A), lambda b, c: (row(b, c), 0))
    vec_spec = pl.BlockSpec((1, INNER_A), lambda b, c: (0, 0))
    return pl.pallas_call(
        _mlstm_kernel,
        grid=(BATCH, steps),
        in_specs=[act_spec, act_spec, act_spec, act_spec,
                  pl.BlockSpec((L, INNER_A), lambda b, c: (row(b, c), 1)),
                  pl.BlockSpec((L, INNER_A), lambda b, c: (row(b, c), 2)),
                  pl.BlockSpec((HEADS_A, L, CHUNK_A), lambda b, c: (0, row(b, c), 0)),
                  pl.BlockSpec((L, LANES), lambda b, c: (row(b, c), 0)),
                  vec_spec, vec_spec],
        out_specs=act_spec,
        out_shape=jax.ShapeDtypeStruct((TOKENS, INNER_A), BF16),
        scratch_shapes=[pltpu.VMEM((HEADS_A, dh, dh), F32), pltpu.VMEM((HEADS_A, 1, dh), F32)],
        compiler_params=pltpu.CompilerParams(
            dimension_semantics=("parallel", "arbitrary"), vmem_limit_bytes=VMEM_LIMIT),
        name=name,
    )(q, k, v, xc, proj, proj, wi, cv, gn, skip)


def _out_ln_kernel(y_ref, w_ref, x_ref, g_ref, b_ref, *rest, with_perm):
    n_out = 1 + (N_GROUPS_B if with_perm else 0)
    outs, scratch = rest[:n_out], rest[n_out:]
    r = ALPHA * x_ref[...] + _dot(y_ref[...], w_ref[...])
    slabs = scratch if with_perm else (None, None)
    _store_stream(_layer_norm_rows(r, g_ref[...], b_ref[...]), outs[0], outs[1:], *slabs)


def _out_ln(y, w, x, g, b, name, with_perm, tm=512):
    m, k = y.shape
    n = w.shape[1]
    out_shape, out_specs = _stream_out(tm, with_perm)
    return pl.pallas_call(
        functools.partial(_out_ln_kernel, with_perm=with_perm),
        grid=(m // tm,),
        in_specs=[pl.BlockSpec((tm, k), lambda i: (i, 0)),
                  pl.BlockSpec((k, n), lambda i: (0, 0)),
                  pl.BlockSpec((tm, n), lambda i: (i, 0)),
                  pl.BlockSpec((1, n), lambda i: (0, 0)),
                  pl.BlockSpec((1, n), lambda i: (0, 0))],
        out_specs=out_specs,
        out_shape=out_shape,
        scratch_shapes=[pltpu.VMEM((N_SLABS, tm, LANES), F32)] * 2 if with_perm else [],
        compiler_params=pltpu.CompilerParams(dimension_semantics=("parallel",),
                                             vmem_limit_bytes=VMEM_LIMIT),
        name=name,
    )(y, w, x, g, b)


def _attn_kernel(q_ref, kvp_ref, kvc_ref, o_ref, lse_ref, *, dilation, blocks_per_seq):
    n = BLOCK_B
    n_blocks = q_ref.shape[0] // n
    qi = lax.broadcasted_iota(jnp.int32, (n, n), 0)
    kj = lax.broadcasted_iota(jnp.int32, (n, n), 1)
    lower = kj <= qi
    diag = kj == qi
    dist = jnp.bitwise_and(qi - kj, n - 1).astype(F32)
    heads = range(HEADS_B)
    ksl = [slice(h * HEAD_DIM_B, (h + 1) * HEAD_DIM_B) for h in heads]
    vsl = [slice(INNER_B + h * HEAD_DIM_B, INNER_B + (h + 1) * HEAD_DIM_B) for h in heads]
    slopes = [2.0 ** (-8.0 * (h + 1.0) / HEADS_B) * dilation for h in heads]

    for blk in range(n_blocks):
        rows = slice(blk * n, (blk + 1) * n)
        prev_ref, prev_rows = ((kvp_ref, slice(0, n)) if blk == 0
                               else (kvc_ref, slice((blk - 1) * n, blk * n)))
        first = (pl.program_id(0) * n_blocks + blk) % blocks_per_seq == 0
        prev_bias = jnp.where(first, -jnp.inf, 0.0)

        both_rows = slice((blk - 1) * n, (blk + 1) * n)
        scores = []
        for h in heads:
            qh = q_ref[rows, ksl[h]]
            if blk == 0:
                s_prev = _dot_nt(qh, prev_ref[prev_rows, ksl[h]])
                s_cur = _dot_nt(qh, kvc_ref[rows, ksl[h]])
            else:
                s_both = _dot_nt(qh, kvc_ref[both_rows, ksl[h]])
                s_prev, s_cur = s_both[:, :n], s_both[:, n:]
            s_prev = s_prev + prev_bias
            s = jnp.where(lower, s_cur, s_prev) - slopes[h] * dist
            s_diag = (jnp.max(jnp.where(diag, s_prev, -jnp.inf), axis=-1, keepdims=True)
                      - slopes[h] * float(n))
            m = jnp.maximum(jnp.max(s, axis=-1, keepdims=True), s_diag)
            scores.append((s, s_diag, m))
        probs = []
        for h in heads:
            s, s_diag, m = scores[h]
            e = jnp.exp(s - m)
            e_diag = jnp.exp(s_diag - m)
            den = jnp.sum(e, axis=-1, keepdims=True) + e_diag
            p_prev = jnp.where(lower, jnp.where(diag, e_diag, 0.0), e)
            probs.append((jnp.where(lower, e, 0.0).astype(BF16), p_prev.astype(BF16), den))
        stats = jnp.ones((n, LANES), F32)
        for h in heads:
            p_cur, p_prev, den = probs[h]
            if blk == 0:
                o = _dot(p_cur, kvc_ref[rows, vsl[h]]) + _dot(p_prev, prev_ref[prev_rows, vsl[h]])
            else:
                o = _dot(jnp.concatenate([p_prev, p_cur], axis=1), kvc_ref[both_rows, vsl[h]])
            o_ref[rows, ksl[h]] = o.astype(BF16)
            stats = jnp.where(kj == h, den, jnp.where(kj == HEADS_B + h, scores[h][2], stats))
        lse_ref[rows, :] = stats


def _attn_group(q, kv, group, name, blocks_per_step=8):
    dilation = DILATIONS[group]
    blocks_per_seq = SEQ // dilation // BLOCK_B
    tq = blocks_per_step * BLOCK_B
    return pl.pallas_call(
        functools.partial(_attn_kernel, dilation=float(dilation), blocks_per_seq=blocks_per_seq),
        grid=(TOKENS // tq,),
        in_specs=[pl.BlockSpec((tq, INNER_B), lambda n: (n, 0)),
                  pl.BlockSpec((BLOCK_B, 2 * INNER_B),
                               lambda n: (jnp.maximum(n * blocks_per_step - 1, 0), 0)),
                  pl.BlockSpec((tq, 2 * INNER_B), lambda n: (n, 0))],
        out_specs=[pl.BlockSpec((tq, INNER_B), lambda n: (n, 0)),
                   pl.BlockSpec((tq, LANES), lambda n: (n, 0))],
        out_shape=[jax.ShapeDtypeStruct((TOKENS, INNER_B), BF16),
                   jax.ShapeDtypeStruct((TOKENS, LANES), F32)],
        compiler_params=pltpu.CompilerParams(dimension_semantics=("arbitrary",),
                                             vmem_limit_bytes=VMEM_LIMIT),
        name=name,
    )(q, kv, kv)


def _merge_out_ln_kernel(o0_ref, o1_ref, o2_ref, l0_ref, l1_ref, l2_ref, z_ref, w_ref, x_ref,
                         g_ref, b_ref, *rest, with_perm):
    n_out = 1 + (N_GROUPS_B if with_perm else 0)
    outs = rest[:n_out]
    y_ref, os1_ref, os2_ref, tmp_ref, ls1_ref, ls2_ref, ltmp_ref = rest[n_out:]
    tm = x_ref.shape[0]
    q4, q16 = tm // RES_STEP, tm // (RES_STEP * RES_STEP)
    for r4 in range(RES_STEP):
        ls1_ref[pl.ds(r4, q4, stride=RES_STEP), :] = l1_ref[r4]
        for a in range(RES_STEP):
            ltmp_ref[pl.ds(r4 * q4 + a, q16, stride=RES_STEP), :] = l2_ref[r4 + RES_STEP * a]
        ls2_ref[pl.ds(r4, q4, stride=RES_STEP), :] = ltmp_ref[r4 * q4:(r4 + 1) * q4, :]
    for s in range(N_SLABS):
        lanes = slice(s * LANES, (s + 1) * LANES)
        for r4 in range(RES_STEP):
            os1_ref[s, pl.ds(r4, q4, stride=RES_STEP), :] = o1_ref[r4, :, lanes].astype(F32)
            for a in range(RES_STEP):
                tmp_ref[s, pl.ds(r4 * q4 + a, q16, stride=RES_STEP), :] = (
                    o2_ref[r4 + RES_STEP * a, :, lanes].astype(F32))
            os2_ref[s, pl.ds(r4, q4, stride=RES_STEP), :] = tmp_ref[s, r4 * q4:(r4 + 1) * q4, :]
    stats = (l0_ref[...], ls1_ref[...], ls2_ref[...])
    maxes = [pltpu.roll(t, LANES - HEADS_B, 1) for t in stats]
    head_lane = lax.broadcasted_iota(jnp.int32, (1, LANES), 1) < HEADS_B
    lses = [m + jnp.log(jnp.where(head_lane, t, 1.0)) for m, t in zip(maxes, stats)]
    mx = jnp.maximum(jnp.maximum(lses[0], lses[1]), lses[2])
    inv = 1.0 / (jnp.exp(lses[0] - mx) + jnp.exp(lses[1] - mx) + jnp.exp(lses[2] - mx))
    w0, w1, w2 = [jnp.exp(m - mx) * inv for m in maxes]
    for h in range(HEADS_B):
        sl = slice(h * HEAD_DIM_B, (h + 1) * HEAD_DIM_B)
        o = (w0[:, h:h + 1] * o0_ref[:, sl].astype(F32) + w1[:, h:h + 1] * os1_ref[h]
             + w2[:, h:h + 1] * os2_ref[h])
        zz = z_ref[:, sl].astype(F32)
        y_ref[:, sl] = (o * (zz * _sigmoid(zz))).astype(BF16)
    r = ALPHA * x_ref[...] + _dot(y_ref[...], w_ref[...])
    _store_stream(_layer_norm_rows(r, g_ref[...], b_ref[...]), outs[0], outs[1:], os1_ref, os2_ref)


def _merge_out_ln(os_, lses, z, w, x, g, b, name, with_perm, tm=512):
    m = x.shape[0]
    n = D_MODEL
    per_seq = SEQ // tm
    tok = lambda width: pl.BlockSpec((tm, width), lambda i: (i, 0))
    res = lambda d, width: pl.BlockSpec((None, d, tm // d, width),
                                        lambda i: (i // per_seq, 0, i % per_seq, 0))
    vec = pl.BlockSpec((1, n), lambda i: (0, 0))
    d1, d2 = DILATIONS[1], DILATIONS[2]
    out_shape, out_specs = _stream_out(tm, with_perm)
    return pl.pallas_call(
        functools.partial(_merge_out_ln_kernel, with_perm=with_perm),
        grid=(m // tm,),
        in_specs=[tok(INNER_B), res(d1, INNER_B), res(d2, INNER_B),
                  tok(LANES), res(d1, LANES), res(d2, LANES),
                  tok(INNER_B),
                  pl.BlockSpec((INNER_B, n), lambda i: (0, 0)),
                  tok(n), vec, vec],
        out_specs=out_specs,
        out_shape=out_shape,
        scratch_shapes=[pltpu.VMEM((tm, INNER_B), BF16),
                        pltpu.VMEM((N_SLABS, tm, LANES), F32), pltpu.VMEM((N_SLABS, tm, LANES), F32),
                        pltpu.VMEM((N_SLABS, tm, LANES), F32),
                        pltpu.VMEM((tm, LANES), F32), pltpu.VMEM((tm, LANES), F32),
                        pltpu.VMEM((tm, LANES), F32)],
        compiler_params=pltpu.CompilerParams(dimension_semantics=("parallel",),
                                             vmem_limit_bytes=VMEM_LIMIT),
        name=name,
    )(os_[0], os_[1].reshape(BATCH, d1, SEQ // d1, INNER_B), os_[2].reshape(BATCH, d2, SEQ // d2, INNER_B),
      lses[0], lses[1].reshape(BATCH, d1, SEQ // d1, LANES), lses[2].reshape(BATCH, d2, SEQ // d2, LANES),
      z, w, x, g, b)


def _block_diag_lane_blocks(w):
    per = LANES // QKV_BLOCK
    w4 = w.reshape(-1, per, QKV_BLOCK, QKV_BLOCK)
    eye = jnp.eye(per, dtype=w.dtype)
    dense = jnp.einsum('bmij,mp->bmjpi', w4, eye)
    return dense.reshape(-1, LANES, LANES)


def _fold_gate_weights(wq, wk, wv, w_if):
    nblk = INNER_A // QKV_BLOCK
    wif = w_if.reshape(3, nblk, QKV_BLOCK, 2 * HEADS_A)
    hp = lax.Precision.HIGHEST
    wgc = (jnp.einsum('nij,nio->njo', wq, wif[0], precision=hp)
           + jnp.einsum('nij,nio->njo', wk, wif[1], precision=hp)).reshape(INNER_A, 2 * HEADS_A)
    wgm = jnp.einsum('nij,nio->njo', wv, wif[2], precision=hp).reshape(INNER_A, 2 * HEADS_A)
    pad = ((0, 0), (0, LANES - 2 * HEADS_A))
    return jnp.pad(wgc, pad).astype(BF16), jnp.pad(wgm, pad).astype(BF16)


def kernel(x, ln_g, ln_b, a_w_in, a_conv_w, a_conv_b, a_wq, a_wk, a_wv, a_w_if, a_b_if, a_gn_g, a_skip,
           a_w_out, b_w_kv, b_w_in, b_w_out):
    xs = x.reshape(TOKENS, D_MODEL)
    x_in = xs
    for layer in range(N_A_LAYERS):
        bdqk =jnp.concatenate([_block_diag_lane_blocks(a_wq[layer]),
                                _block_diag_lane_blocks(a_wk[layer])], axis=-1).astype(BF16)
        bdv = _block_diag_lane_blocks(a_wv[layer]).astype(BF16)
        wgc, wgm = _fold_gate_weights(a_wq[layer], a_wk[layer], a_wv[layer], a_w_if[layer])
        bg = jnp.pad(a_b_if[layer][None, :], ((0, 0), (0, LANES - 2 * HEADS_A)))
        proj = _proj(x_in, a_w_in, layer, 0, 3 * INNER_A, f"a{layer}_proj")
        q, k, v, xc, wi, cv = _mlstm_pre(proj, a_conv_w[layer], a_conv_b[layer][None, :], bdqk, bdv,
                                         wgc, wgm, bg, f"a{layer}_pre")
        y = _mlstm_core(q, k, v, xc, proj, wi, cv, a_gn_g[layer][None, :], a_skip[layer][None, :],
                        f"a{layer}_mlstm")
        outs = _out_ln(y, a_w_out[layer].astype(BF16), xs, ln_g[layer][None, :], ln_b[layer][None, :],
                       f"a{layer}_out_ln", with_perm=layer == N_A_LAYERS - 1)
        xs = outs[0]
        x_in = xs

    perms = [p.reshape(TOKENS, D_MODEL) for p in outs[1:]]
    kvs = [_proj(perms[g], b_w_kv[None], 0, 2 * g * INNER_B, 2 * INNER_B, f"kv_proj{g}")
           for g in range(N_GROUPS_B)]
    q_scale = HEAD_DIM_B ** -0.5
    for lb in range(N_B_LAYERS):
        layer = N_A_LAYERS + lb
        qs = [_proj(perms[g], b_w_in, lb, g * INNER_B, INNER_B, f"b{lb}_proj{g}", out_scale=q_scale)
              for g in range(N_GROUPS_B)]
        z = _proj(perms[0], b_w_in, lb, N_GROUPS_B * INNER_B, INNER_B, f"b{lb}_projz")
        os_, lses = [], []
        for g in range(N_GROUPS_B):
            o, lse = _attn_group(qs[g], kvs[g], g, f"b{lb}_attn{g}")
            os_.append(o)
            lses.append(lse)
        outs = _merge_out_ln(os_, lses, z, b_w_out[lb].astype(BF16), xs, ln_g[layer][None, :],
                             ln_b[layer][None, :], f"b{lb}_merge_out_ln",
                             with_perm=lb < N_B_LAYERS - 1)
        xs = outs[0]
        perms = [p.reshape(TOKENS, D_MODEL) for p in outs[1:]]
    return xs.reshape(BATCH, SEQ, D_MODEL)
```

```python
import functools

import jax
import jax.numpy as jnp
from jax import lax
from jax.experimental import pallas as pl
from jax.experimental.pallas import tpu as pltpu

D_MODEL = 1024
BATCH = 4
SEQ = 4096
DEPTH = 4
N_A_LAYERS = DEPTH // 2
N_B_LAYERS = DEPTH - N_A_LAYERS
INNER_A = 2 * D_MODEL
HEADS_A = 4
HEAD_DIM_A = INNER_A // HEADS_A
QKV_BLOCK = 4
CONV_K = 4
HEAD_DIM_B = 128
HEADS_B = D_MODEL // HEAD_DIM_B
INNER_B = HEADS_B * HEAD_DIM_B
GROUPS_B = ((128, 1), (512, 4), (2048, 16))
N_GROUPS_B = len(GROUPS_B)
DILATIONS = tuple(d for _, d in GROUPS_B)
BLOCK_B = 128
RES_STEP = 4
assert DILATIONS == (1, RES_STEP, RES_STEP * RES_STEP)
ALPHA = (2 * DEPTH) ** 0.25
LN_EPS = 1e-5

TOKENS = BATCH * SEQ
LANES = 128
SUBLANES = 8
N_SLABS = D_MODEL // LANES
CHUNK_A = 256
N_LANE_BLOCKS_A = INNER_A // LANES
VMEM_LIMIT = 48 * 1024 * 1024
PROJ_TN = 1024
CV_W_INTER, CV_FLOOR, CV_W_STATE, CV_DECAY = 0, HEADS_A, 2 * HEADS_A, 3 * HEADS_A

F32 = jnp.float32
BF16 = jnp.bfloat16

assert all(w // d == BLOCK_B for w, d in GROUPS_B)


def _dot(a, b):
    return jnp.dot(a, b, preferred_element_type=F32)


def _dot_nt(a, b):
    return lax.dot_general(a, b, (((1,), (1,)), ((), ())), preferred_element_type=F32)


def _sigmoid(x):
    return 0.5 * jnp.tanh(0.5 * x) + 0.5


def _layer_norm_rows(r, g, b):
    mu = jnp.mean(r, axis=-1, keepdims=True)
    d = r - mu
    var = jnp.mean(d * d, axis=-1, keepdims=True)
    return d * lax.rsqrt(var + LN_EPS) * g + b


def _store_stream(r, x_ref, perm_refs, slab_ref, slab4_ref):
    x_ref[...] = r
    if not perm_refs:
        return
    xb1_ref, xb4_ref, xb16_ref = perm_refs
    xb1_ref[...] = r.astype(BF16)
    tm = r.shape[0]
    q4, q16 = tm // RES_STEP, tm // (RES_STEP * RES_STEP)
    for s in range(N_SLABS):
        lanes = slice(s * LANES, (s + 1) * LANES)
        slab_ref[s] = r[:, lanes]
        for r4 in range(RES_STEP):
            part = slab_ref[s, pl.ds(r4, q4, stride=RES_STEP), :]
            slab4_ref[s, r4 * q4:(r4 + 1) * q4, :] = part
            xb4_ref[r4, :, lanes] = part.astype(BF16)
        for r4 in range(RES_STEP):
            for a in range(RES_STEP):
                xb16_ref[r4 + RES_STEP * a, :, lanes] = (
                    slab4_ref[s, pl.ds(r4 * q4 + a, q16, stride=RES_STEP), :].astype(BF16))


def _stream_out(tm, with_perm):
    per_seq = SEQ // tm
    shapes = [jax.ShapeDtypeStruct((TOKENS, D_MODEL), F32)]
    specs = [pl.BlockSpec((tm, D_MODEL), lambda i: (i, 0))]
    if with_perm:
        shapes.append(jax.ShapeDtypeStruct((TOKENS, D_MODEL), BF16))
        specs.append(pl.BlockSpec((tm, D_MODEL), lambda i: (i, 0)))
        for d in DILATIONS[1:]:
            shapes.append(jax.ShapeDtypeStruct((BATCH, d, SEQ // d, D_MODEL), BF16))
            specs.append(pl.BlockSpec((None, d, tm // d, D_MODEL),
                                      lambda i: (i // per_seq, 0, i % per_seq, 0)))
    return shapes, specs


def _proj_kernel(x_ref, w_ref, o_ref, wb_ref, *, out_scale):
    @pl.when(pl.program_id(1) == 0)
    def _():
        wb_ref[...] = w_ref[...].astype(BF16)

    acc = _dot(x_ref[...].astype(BF16), wb_ref[...])
    if out_scale is not None:
        acc = acc * out_scale
    o_ref[...] = acc.astype(o_ref.dtype)


def _proj(x, w, layer, col0, n, name, out_scale=None, tm=2048, tn=PROJ_TN):
    m, k = x.shape
    col_blk0 = col0 // tn
    return pl.pallas_call(
        functools.partial(_proj_kernel, out_scale=out_scale),
        grid=(n // tn, m // tm),
        in_specs=[pl.BlockSpec((tm, k), lambda j, i: (i, 0)),
                  pl.BlockSpec((None, k, tn), lambda j, i: (layer, 0, col_blk0 + j))],
        out_specs=pl.BlockSpec((tm, tn), lambda j, i: (i, j)),
        out_shape=jax.ShapeDtypeStruct((m, n), BF16),
        scratch_shapes=[pltpu.VMEM((k, tn), BF16)],
        compiler_params=pltpu.CompilerParams(dimension_semantics=("parallel", "arbitrary"),
                                             vmem_limit_bytes=VMEM_LIMIT),
        name=name,
    )(x, w)


def _pre_kernel(xm_ref, halo_ref, cw_ref, cb_ref, bdqk_ref, bdv_ref, wgc_ref, wgm_ref, bg_ref,
                q_ref, k_ref, v_ref, xc_ref, wi_ref, cv_ref, ext_ref, m_ref, gates_ref, *, n_tiles):
    tm = xm_ref.shape[0]
    i = pl.program_id(0)
    tile = jnp.minimum(i, n_tiles - 1)
    seq_start = (tile * tm) % SEQ == 0
    prev_seq_start = ((i - 1) * tm) % SEQ == 0

    @pl.when(i == 0)
    def _():
        gates_ref[...] = jnp.zeros_like(gates_ref)
        m_ref[...] = jnp.zeros_like(m_ref)

    m_start = jnp.where(prev_seq_start, 0.0, m_ref[0:1, :])
    m_ref[0:1, :] = _gate_weights(gates_ref[...], m_start, wi_ref, cv_ref)

    halo = jnp.where(seq_start, 0.0, halo_ref[...].astype(F32))
    for blk in range(N_LANE_BLOCKS_A):
        sl = slice(blk * LANES, (blk + 1) * LANES)
        ext_ref[blk, pl.ds(0, SUBLANES, stride=2), :] = halo[:, sl]
        ext_ref[blk, pl.ds(2 * SUBLANES, tm, stride=2), :] = xm_ref[:, sl].astype(F32)
        acc = cb_ref[:, sl]
        for j in range(CONV_K):
            off = 2 * (SUBLANES - (CONV_K - 1) + j)
            acc = acc + cw_ref[j:j + 1, sl] * ext_ref[blk, pl.ds(off, tm, stride=2), :]
        xc_ref[:, sl] = (acc * _sigmoid(acc)).astype(BF16)
        qk = _dot(xc_ref[:, sl], bdqk_ref[blk])
        q_ref[:, sl] = qk[:, :LANES].astype(BF16)
        k_ref[:, sl] = qk[:, LANES:].astype(BF16)
        v_ref[:, sl] = _dot(xm_ref[:, sl], bdv_ref[blk]).astype(BF16)

    gates_ref[...] = _dot(xc_ref[...], wgc_ref[...]) + _dot(xm_ref[...], wgm_ref[...]) + bg_ref[...]


def _gate_weights(gates, m_start, wi_ref, cv_ref):
    tm = gates.shape[0]
    L = CHUNK_A
    scale = HEAD_DIM_A ** -0.5
    log_sig = jnp.minimum(gates, 0.0) - jnp.log(1.0 + jnp.exp(-jnp.abs(gates)))
    log_f = pltpu.roll(log_sig, LANES - HEADS_A, 1)
    row = lax.broadcasted_iota(jnp.int32, (L, L), 0)
    col = lax.broadcasted_iota(jnp.int32, (L, L), 1)
    tri = col <= row
    tri_b = jnp.where(tri, 1.0, 0.0).astype(BF16)
    lane = lax.broadcasted_iota(jnp.int32, (L, LANES), 1)
    head_lane = lax.broadcasted_iota(jnp.int32, (1, LANES), 1) < HEADS_A
    chunks = [slice(c * L, (c + 1) * L) for c in range(tm // L)]

    parts = []
    for rows in chunks:
        li = gates[rows, :]
        lf = log_f[rows, :]
        lf_hi = lf.astype(BF16)
        lf_mid = (lf - lf_hi.astype(F32)).astype(BF16)
        lf_lo = (lf - lf_hi.astype(F32) - lf_mid.astype(F32)).astype(BF16)
        b = _dot(tri_b, lf_hi) + _dot(tri_b, lf_mid) + _dot(tri_b, lf_lo)
        g_tot = b[L - 1:L, :]
        li_rows = li.T[0:SUBLANES, :]
        b_rows = b.T[0:SUBLANES, :]
        row_max = jnp.zeros((L, LANES), F32)
        for h in range(HEADS_A):
            d_intra = jnp.where(tri, b[:, h:h + 1] - b_rows[h:h + 1, :] + li_rows[h:h + 1, :], -jnp.inf)
            wi_ref[h, rows, :] = d_intra
            row_max = jnp.where(lane == h, jnp.max(d_intra, axis=1, keepdims=True), row_max)
        lw = g_tot - b + li
        parts.append((b, g_tot, lw, jnp.max(lw, axis=0, keepdims=True), row_max))

    m_list = [m_start]
    for b, g_tot, lw, lw_max, row_max in parts:
        m_list.append(jnp.where(head_lane, jnp.maximum(g_tot + m_list[-1], lw_max), 0.0))

    for c, rows in enumerate(chunks):
        b, g_tot, lw, lw_max, row_max = parts[c]
        m_prev, m_new = m_list[c], m_list[c + 1]
        a_inter = b + m_prev
        m_t = jnp.maximum(a_inter, row_max)
        for h in range(HEADS_A):
            wi_ref[h, rows, :] = jnp.exp(wi_ref[h, rows, :] - m_t[:, h:h + 1]) * scale
        w_inter = jnp.exp(a_inter - m_t)
        floor = jnp.exp(-m_t)
        w_state = jnp.exp(lw - m_new) * scale
        decay = jnp.broadcast_to(jnp.exp(g_tot + m_prev - m_new), (L, LANES))
        cv_ref[rows, :] = jnp.where(
            lane < CV_FLOOR, w_inter,
            jnp.where(lane < CV_W_STATE, pltpu.roll(floor, CV_FLOOR, 1),
                      jnp.where(lane < CV_DECAY, pltpu.roll(w_state, CV_W_STATE, 1),
                                jnp.where(lane < CV_DECAY + HEADS_A, pltpu.roll(decay, CV_DECAY, 1),
                                          0.0))))
    return m_list[-1]


def _mlstm_pre(proj, cw, cb, bdqk, bdv, wgc, wgm, bg, name, tm=512):
    n_rows = proj.shape[0]
    n_tiles = n_rows // tm
    act = jax.ShapeDtypeStruct((n_rows, INNER_A), BF16)
    cur = lambda i: jnp.minimum(i, n_tiles - 1)
    prev = lambda i: jnp.maximum(i - 1, 0)
    row_spec = pl.BlockSpec((tm, INNER_A), lambda i: (cur(i), 0))
    full = lambda shape: pl.BlockSpec(shape, lambda i: (0,) * len(shape))
    return pl.pallas_call(
        functools.partial(_pre_kernel, n_tiles=n_tiles),
        grid=(n_tiles + 1,),
        in_specs=[row_spec,
                  pl.BlockSpec((SUBLANES, INNER_A),
                               lambda i: (jnp.maximum(cur(i) * (tm // SUBLANES) - 1, 0), 0)),
                  full((CONV_K, INNER_A)), full((1, INNER_A)),
                  full((N_LANE_BLOCKS_A, LANES, 2 * LANES)), full((N_LANE_BLOCKS_A, LANES, LANES)),
                  full((INNER_A, LANES)), full((INNER_A, LANES)), full((1, LANES))],
        out_specs=[row_spec, row_spec, row_spec, row_spec,
                   pl.BlockSpec((HEADS_A, tm, CHUNK_A), lambda i: (0, prev(i), 0)),
                   pl.BlockSpec((tm, LANES), lambda i: (prev(i), 0))],
        out_shape=[act, act, act, act,
                   jax.ShapeDtypeStruct((HEADS_A, n_rows, CHUNK_A), F32),
                   jax.ShapeDtypeStruct((n_rows, LANES), F32)],
        scratch_shapes=[pltpu.VMEM((N_LANE_BLOCKS_A, 2 * (tm + SUBLANES), LANES), F32),
                        pltpu.VMEM((SUBLANES, LANES), F32), pltpu.VMEM((tm, LANES), F32)],
        compiler_params=pltpu.CompilerParams(dimension_semantics=("arbitrary",),
                                             vmem_limit_bytes=VMEM_LIMIT),
        name=name,
    )(proj, proj, cw, cb, bdqk, bdv, wgc, wgm, bg)


def _mlstm_kernel(q_ref, k_ref, v_ref, xc_ref, z_ref, o_ref, wi_ref, cv_ref, gn_ref, sk_ref, y_ref,
                  ct_ref, n_ref):
    @pl.when(pl.program_id(1) == 0)
    def _():
        ct_ref[...] = jnp.zeros_like(ct_ref)
        n_ref[...] = jnp.zeros_like(n_ref)

    for c in range(q_ref.shape[0] // CHUNK_A):
        r = slice(c * CHUNK_A, (c + 1) * CHUNK_A)
        for h in range(HEADS_A):
            hs = slice(h * HEAD_DIM_A, (h + 1) * HEAD_DIM_A)
            qb = q_ref[r, hs]
            kb = k_ref[r, hs]
            vb = v_ref[r, hs]
            w_inter = cv_ref[r, CV_W_INTER + h:CV_W_INTER + h + 1]
            floor = cv_ref[r, CV_FLOOR + h:CV_FLOOR + h + 1]
            w_state = cv_ref[r, CV_W_STATE + h:CV_W_STATE + h + 1]
            decay = cv_ref[c * CHUNK_A:c * CHUNK_A + 1, CV_DECAY + h:CV_DECAY + h + 1]

            sc = _dot_nt(qb, kb) * wi_ref[h, r, :]
            num = w_inter * _dot(qb, ct_ref[h].astype(BF16)) + _dot(sc.astype(BF16), vb)
            n_rows = jnp.broadcast_to(n_ref[h].astype(BF16), (LANES, HEAD_DIM_A))
            den = w_inter * _dot_nt(qb, n_rows)[:, 0:1] + jnp.sum(sc, axis=1, keepdims=True)
            h_tilde = num * (1.0 / jnp.maximum(jnp.abs(den), floor))

            kw = kb.astype(F32) * w_state
            ct_ref[h] = decay * ct_ref[h] + _dot(kw.T.astype(BF16), vb)
            n_ref[h] = decay * n_ref[h] + jnp.sum(kw, axis=0, keepdims=True)

            hg = h_tilde * _sigmoid(o_ref[r, hs].astype(F32))
            mu = jnp.mean(hg, axis=1, keepdims=True)
            dlt = hg - mu
            var = jnp.mean(dlt * dlt, axis=1, keepdims=True)
            hn = dlt * lax.rsqrt(var + LN_EPS) * gn_ref[:, hs]
            zz = z_ref[r, hs].astype(F32)
            y = (hn + sk_ref[:, hs] * xc_ref[r, hs].astype(F32)) * (zz * _sigmoid(zz))
            y_ref[r, hs] = y.astype(BF16)


def _mlstm_core(q, k, v, xc, proj, wi, cv, gn, skip, name, chunks_per_step=2):
    L = chunks_per_step * CHUNK_A
    dh = HEAD_DIM_A
    steps = SEQ // L
    row = lambda b, c: b * steps + c
    act_spec = pl.BlockSpec((L, INNER_A), lambda b, c: (row(b, c), 0))
    vec_spec = pl.BlockSpec((1, INNER_A), lambda b, c: (0, 0))
    return pl.pallas_call(
        _mlstm_kernel,
        grid=(BATCH, steps),
        in_specs=[act_spec, act_spec, act_spec, act_spec,
                  pl.BlockSpec((L, INNER_A), lambda b, c: (row(b, c), 1)),
                  pl.BlockSpec((L, INNER_A), lambda b, c: (row(b, c), 2)),
                  pl.BlockSpec((HEADS_A, L, CHUNK_A), lambda b, c: (0, row(b, c), 0)),
                  pl.BlockSpec((L, LANES), lambda b, c: (row(b, c), 0)),
                  vec_spec, vec_spec],
        out_specs=act_spec,
        out_shape=jax.ShapeDtypeStruct((TOKENS, INNER_A), BF16),
        scratch_shapes=[pltpu.VMEM((HEADS_A, dh, dh), F32), pltpu.VMEM((HEADS_A, 1, dh), F32)],
        compiler_params=pltpu.CompilerParams(
            dimension_semantics=("parallel", "arbitrary"), vmem_limit_bytes=VMEM_LIMIT),
        name=name,
    )(q, k, v, xc, proj, proj, wi, cv, gn, skip)


def _out_ln_kernel(y_ref, w_ref, x_ref, g_ref, b_ref, *rest, with_perm):
    n_out = 1 + (N_GROUPS_B if with_perm else 0)
    outs, scratch = rest[:n_out], rest[n_out:]
    wb_ref = scratch[0]

    @pl.when(pl.program_id(0) == 0)
    def _():
        wb_ref[...] = w_ref[...].astype(BF16)

    r = ALPHA * x_ref[...] + _dot(y_ref[...], wb_ref[...])
    slabs = scratch[1:] if with_perm else (None, None)
    _store_stream(_layer_norm_rows(r, g_ref[...], b_ref[...]), outs[0], outs[1:], *slabs)


def _out_ln(y, w, layer, x, g, b, name, with_perm, tm=512):
    m, k = y.shape
    n = w.shape[2]
    out_shape, out_specs = _stream_out(tm, with_perm)
    return pl.pallas_call(
        functools.partial(_out_ln_kernel, with_perm=with_perm),
        grid=(m // tm,),
        in_specs=[pl.BlockSpec((tm, k), lambda i: (i, 0)),
                  pl.BlockSpec((None, k, n), lambda i: (layer, 0, 0)),
                  pl.BlockSpec((tm, n), lambda i: (i, 0)),
                  pl.BlockSpec((1, n), lambda i: (0, 0)),
                  pl.BlockSpec((1, n), lambda i: (0, 0))],
        out_specs=out_specs,
        out_shape=out_shape,
        scratch_shapes=([pltpu.VMEM((k, n), BF16)]
                        + ([pltpu.VMEM((N_SLABS, tm, LANES), F32)] * 2 if with_perm else [])),
        compiler_params=pltpu.CompilerParams(dimension_semantics=("arbitrary",),
                                             vmem_limit_bytes=VMEM_LIMIT),
        name=name,
    )(y, w, x, g, b)


def _attn_kernel(q_ref, kvp_ref, kvc_ref, o_ref, lse_ref, *, dilation, blocks_per_seq):
    n = BLOCK_B
    n_blocks = q_ref.shape[0] // n
    qi = lax.broadcasted_iota(jnp.int32, (n, n), 0)
    kj = lax.broadcasted_iota(jnp.int32, (n, n), 1)
    lower = kj <= qi
    diag = kj == qi
    dist = jnp.bitwise_and(qi - kj, n - 1).astype(F32)
    heads = range(HEADS_B)
    ksl = [slice(h * HEAD_DIM_B, (h + 1) * HEAD_DIM_B) for h in heads]
    vsl = [slice(INNER_B + h * HEAD_DIM_B, INNER_B + (h + 1) * HEAD_DIM_B) for h in heads]
    slopes = [2.0 ** (-8.0 * (h + 1.0) / HEADS_B) * dilation for h in heads]

    for blk in range(n_blocks):
        rows = slice(blk * n, (blk + 1) * n)
        prev_ref, prev_rows = ((kvp_ref, slice(0, n)) if blk == 0
                               else (kvc_ref, slice((blk - 1) * n, blk * n)))
        first = (pl.program_id(0) * n_blocks + blk) % blocks_per_seq == 0
        prev_bias = jnp.where(first, -jnp.inf, 0.0)

        both_rows = slice((blk - 1) * n, (blk + 1) * n)
        scores = []
        for h in heads:
            qh = q_ref[rows, ksl[h]]
            if blk == 0:
                s_prev = _dot_nt(qh, prev_ref[prev_rows, ksl[h]])
                s_cur = _dot_nt(qh, kvc_ref[rows, ksl[h]])
            else:
                s_both = _dot_nt(qh, kvc_ref[both_rows, ksl[h]])
                s_prev, s_cur = s_both[:, :n], s_both[:, n:]
            s_prev = s_prev + prev_bias
            s = jnp.where(lower, s_cur, s_prev) - slopes[h] * dist
            s_diag = (jnp.max(jnp.where(diag, s_prev, -jnp.inf), axis=-1, keepdims=True)
                      - slopes[h] * float(n))
            m = jnp.maximum(jnp.max(s, axis=-1, keepdims=True), s_diag)
            scores.append((s, s_diag, m))
        probs = []
        for h in heads:
            s, s_diag, m = scores[h]
            e = jnp.exp(s - m)
            e_diag = jnp.exp(s_diag - m)
            den = jnp.sum(e, axis=-1, keepdims=True) + e_diag
            p_prev = jnp.where(lower, jnp.where(diag, e_diag, 0.0), e)
            probs.append((jnp.where(lower, e, 0.0).astype(BF16), p_prev.astype(BF16), den))
        stats = jnp.ones((n, LANES), F32)
        for h in heads:
            p_cur, p_prev, den = probs[h]
            if blk == 0:
                o = _dot(p_cur, kvc_ref[rows, vsl[h]]) + _dot(p_prev, prev_ref[prev_rows, vsl[h]])
            else:
                o = _dot(jnp.concatenate([p_prev, p_cur], axis=1), kvc_ref[both_rows, vsl[h]])
            o_ref[rows, ksl[h]] = o.astype(BF16)
            stats = jnp.where(kj == h, den, jnp.where(kj == HEADS_B + h, scores[h][2], stats))
        lse_ref[rows, :] = stats


def _attn_group(q, kv, group, name, blocks_per_step=8):
    dilation = DILATIONS[group]
    blocks_per_seq = SEQ // dilation // BLOCK_B
    tq = blocks_per_step * BLOCK_B
    return pl.pallas_call(
        functools.partial(_attn_kernel, dilation=float(dilation), blocks_per_seq=blocks_per_seq),
        grid=(TOKENS // tq,),
        in_specs=[pl.BlockSpec((tq, INNER_B), lambda n: (n, 0)),
                  pl.BlockSpec((BLOCK_B, 2 * INNER_B),
                               lambda n: (jnp.maximum(n * blocks_per_step - 1, 0), 0)),
                  pl.BlockSpec((tq, 2 * INNER_B), lambda n: (n, 0))],
        out_specs=[pl.BlockSpec((tq, INNER_B), lambda n: (n, 0)),
                   pl.BlockSpec((tq, LANES), lambda n: (n, 0))],
        out_shape=[jax.ShapeDtypeStruct((TOKENS, INNER_B), BF16),
                   jax.ShapeDtypeStruct((TOKENS, LANES), F32)],
        compiler_params=pltpu.CompilerParams(dimension_semantics=("arbitrary",),
                                             vmem_limit_bytes=VMEM_LIMIT),
        name=name,
    )(q, kv, kv)


def _merge_out_ln_kernel(o0_ref, o1_ref, o2_ref, l0_ref, l1_ref, l2_ref, z_ref, w_ref, x_ref,
                         g_ref, b_ref, *rest, with_perm):
    n_out = 1 + (N_GROUPS_B if with_perm else 0)
    outs = rest[:n_out]
    y_ref, os1_ref, os2_ref, tmp_ref, ls1_ref, ls2_ref, ltmp_ref, wb_ref = rest[n_out:]
    tm = x_ref.shape[0]

    @pl.when(pl.program_id(0) == 0)
    def _():
        wb_ref[...] = w_ref[...].astype(BF16)

    q4, q16 = tm // RES_STEP, tm // (RES_STEP * RES_STEP)
    for r4 in range(RES_STEP):
        ls1_ref[pl.ds(r4, q4, stride=RES_STEP), :] = l1_ref[r4]
        for a in range(RES_STEP):
            ltmp_ref[pl.ds(r4 * q4 + a, q16, stride=RES_STEP), :] = l2_ref[r4 + RES_STEP * a]
        ls2_ref[pl.ds(r4, q4, stride=RES_STEP), :] = ltmp_ref[r4 * q4:(r4 + 1) * q4, :]
    for s in range(N_SLABS):
        lanes = slice(s * LANES, (s + 1) * LANES)
        for r4 in range(RES_STEP):
            os1_ref[s, pl.ds(r4, q4, stride=RES_STEP), :] = o1_ref[r4, :, lanes].astype(F32)
            for a in range(RES_STEP):
                tmp_ref[s, pl.ds(r4 * q4 + a, q16, stride=RES_STEP), :] = (
                    o2_ref[r4 + RES_STEP * a, :, lanes].astype(F32))
            os2_ref[s, pl.ds(r4, q4, stride=RES_STEP), :] = tmp_ref[s, r4 * q4:(r4 + 1) * q4, :]
    stats = (l0_ref[...], ls1_ref[...], ls2_ref[...])
    maxes = [pltpu.roll(t, LANES - HEADS_B, 1) for t in stats]
    head_lane = lax.broadcasted_iota(jnp.int32, (1, LANES), 1) < HEADS_B
    lses = [m + jnp.log(jnp.where(head_lane, t, 1.0)) for m, t in zip(maxes, stats)]
    mx = jnp.maximum(jnp.maximum(lses[0], lses[1]), lses[2])
    inv = 1.0 / (jnp.exp(lses[0] - mx) + jnp.exp(lses[1] - mx) + jnp.exp(lses[2] - mx))
    w0, w1, w2 = [jnp.exp(m - mx) * inv for m in maxes]
    for h in range(HEADS_B):
        sl = slice(h * HEAD_DIM_B, (h + 1) * HEAD_DIM_B)
        o = (w0[:, h:h + 1] * o0_ref[:, sl].astype(F32) + w1[:, h:h + 1] * os1_ref[h]
             + w2[:, h:h + 1] * os2_ref[h])
        zz = z_ref[:, sl].astype(F32)
        y_ref[:, sl] = (o * (zz * _sigmoid(zz))).astype(BF16)
    r = ALPHA * x_ref[...] + _dot(y_ref[...], wb_ref[...])
    _store_stream(_layer_norm_rows(r, g_ref[...], b_ref[...]), outs[0], outs[1:], os1_ref, os2_ref)


def _merge_out_ln(os_, lses, z, w, layer, x, g, b, name, with_perm, tm=512):
    m = x.shape[0]
    n = D_MODEL
    per_seq = SEQ // tm
    tok = lambda width: pl.BlockSpec((tm, width), lambda i: (i, 0))
    res = lambda d, width: pl.BlockSpec((None, d, tm // d, width),
                                        lambda i: (i // per_seq, 0, i % per_seq, 0))
    vec = pl.BlockSpec((1, n), lambda i: (0, 0))
    d1, d2 = DILATIONS[1], DILATIONS[2]
    out_shape, out_specs = _stream_out(tm, with_perm)
    return pl.pallas_call(
        functools.partial(_merge_out_ln_kernel, with_perm=with_perm),
        grid=(m // tm,),
        in_specs=[tok(INNER_B), res(d1, INNER_B), res(d2, INNER_B),
                  tok(LANES), res(d1, LANES), res(d2, LANES),
                  tok(INNER_B),
                  pl.BlockSpec((None, INNER_B, n), lambda i: (layer, 0, 0)),
                  tok(n), vec, vec],
        out_specs=out_specs,
        out_shape=out_shape,
        scratch_shapes=[pltpu.VMEM((tm, INNER_B), BF16),
                        pltpu.VMEM((N_SLABS, tm, LANES), F32), pltpu.VMEM((N_SLABS, tm, LANES), F32),
                        pltpu.VMEM((N_SLABS, tm, LANES), F32),
                        pltpu.VMEM((tm, LANES), F32), pltpu.VMEM((tm, LANES), F32),
                        pltpu.VMEM((tm, LANES), F32), pltpu.VMEM((INNER_B, n), BF16)],
        compiler_params=pltpu.CompilerParams(dimension_semantics=("arbitrary",),
                                             vmem_limit_bytes=VMEM_LIMIT),
        name=name,
    )(os_[0], os_[1].reshape(BATCH, d1, SEQ // d1, INNER_B), os_[2].reshape(BATCH, d2, SEQ // d2, INNER_B),
      lses[0], lses[1].reshape(BATCH, d1, SEQ // d1, LANES), lses[2].reshape(BATCH, d2, SEQ // d2, LANES),
      z, w, x, g, b)


def _block_diag_lane_blocks(w):
    per = LANES // QKV_BLOCK
    w4 = w.reshape(-1, per, QKV_BLOCK, QKV_BLOCK)
    eye = jnp.eye(per, dtype=w.dtype)
    dense = jnp.einsum('bmij,mp->bmjpi', w4, eye)
    return dense.reshape(-1, LANES, LANES)


def _fold_gate_weights(wq, wk, wv, w_if):
    nblk = INNER_A // QKV_BLOCK
    wif = w_if.reshape(3, nblk, QKV_BLOCK, 2 * HEADS_A)
    hp = lax.Precision.HIGHEST
    wgc = (jnp.einsum('nij,nio->njo', wq, wif[0], precision=hp)
           + jnp.einsum('nij,nio->njo', wk, wif[1], precision=hp)).reshape(INNER_A, 2 * HEADS_A)
    wgm = jnp.einsum('nij,nio->njo', wv, wif[2], precision=hp).reshape(INNER_A, 2 * HEADS_A)
    pad = ((0, 0), (0, LANES - 2 * HEADS_A))
    return jnp.pad(wgc, pad).astype(BF16), jnp.pad(wgm, pad).astype(BF16)


def kernel(x, ln_g, ln_b, a_w_in, a_conv_w, a_conv_b, a_wq, a_wk, a_wv, a_w_if, a_b_if, a_gn_g, a_skip,
           a_w_out, b_w_kv, b_w_in, b_w_out):
    xs = x.reshape(TOKENS, D_MODEL)
    x_in = xs
    for layer in range(N_A_LAYERS):
        bdqk =jnp.concatenate([_block_diag_lane_blocks(a_wq[layer]),
                                _block_diag_lane_blocks(a_wk[layer])], axis=-1).astype(BF16)
        bdv = _block_diag_lane_blocks(a_wv[layer]).astype(BF16)
        wgc, wgm = _fold_gate_weights(a_wq[layer], a_wk[layer], a_wv[layer], a_w_if[layer])
        bg = jnp.pad(a_b_if[layer][None, :], ((0, 0), (0, LANES - 2 * HEADS_A)))
        proj = _proj(x_in, a_w_in, layer, 0, 3 * INNER_A, f"a{layer}_proj")
        q, k, v, xc, wi, cv = _mlstm_pre(proj, a_conv_w[layer], a_conv_b[layer][None, :], bdqk, bdv,
                                         wgc, wgm, bg, f"a{layer}_pre")
        y = _mlstm_core(q, k, v, xc, proj, wi, cv, a_gn_g[layer][None, :], a_skip[layer][None, :],
                        f"a{layer}_mlstm")
        outs = _out_ln(y, a_w_out, layer, xs, ln_g[layer][None, :], ln_b[layer][None, :],
                       f"a{layer}_out_ln", with_perm=layer == N_A_LAYERS - 1)
        xs = outs[0]
        x_in = xs

    perms = [p.reshape(TOKENS, D_MODEL) for p in outs[1:]]
    kvs = [_proj(perms[g], b_w_kv[None], 0, 2 * g * INNER_B, 2 * INNER_B, f"kv_proj{g}")
           for g in range(N_GROUPS_B)]
    q_scale = HEAD_DIM_B ** -0.5
    for lb in range(N_B_LAYERS):
        layer = N_A_LAYERS + lb
        qs = [_proj(perms[g], b_w_in, lb, g * INNER_B, INNER_B, f"b{lb}_proj{g}", out_scale=q_scale)
              for g in range(N_GROUPS_B)]
        z = _proj(perms[0], b_w_in, lb, N_GROUPS_B * INNER_B, INNER_B, f"b{lb}_projz")
        os_, lses = [], []
        for g in range(N_GROUPS_B):
            o, lse = _attn_group(qs[g], kvs[g], g, f"b{lb}_attn{g}")
            os_.append(o)
            lses.append(lse)
        outs = _merge_out_ln(os_, lses, z, b_w_out, lb, xs, ln_g[layer][None, :],
                             ln_b[layer][None, :], f"b{lb}_merge_out_ln",
                             with_perm=lb < N_B_LAYERS - 1)
        xs = outs[0]
        perms = [p.reshape(TOKENS, D_MODEL) for p in outs[1:]]
    return xs.reshape(BATCH, SEQ, D_MODEL)
```

```python
import functools

import jax
import jax.numpy as jnp
from jax import lax
from jax.experimental import pallas as pl
from jax.experimental.pallas import tpu as pltpu

D_MODEL = 1024
BATCH = 4
SEQ = 4096
DEPTH = 4
N_A_LAYERS = DEPTH // 2
N_B_LAYERS = DEPTH - N_A_LAYERS
INNER_A = 2 * D_MODEL
HEADS_A = 4
HEAD_DIM_A = INNER_A // HEADS_A
QKV_BLOCK = 4
CONV_K = 4
HEAD_DIM_B = 128
HEADS_B = D_MODEL // HEAD_DIM_B
INNER_B = HEADS_B * HEAD_DIM_B
GROUPS_B = ((128, 1), (512, 4), (2048, 16))
N_GROUPS_B = len(GROUPS_B)
DILATIONS = tuple(d for _, d in GROUPS_B)
BLOCK_B = 128
RES_STEP = 4
assert DILATIONS == (1, RES_STEP, RES_STEP * RES_STEP)
ALPHA = (2 * DEPTH) ** 0.25
LN_EPS = 1e-5

TOKENS = BATCH * SEQ
LANES = 128
SUBLANES = 8
N_SLABS = D_MODEL // LANES
CHUNK_A = 256
N_LANE_BLOCKS_A = INNER_A // LANES
VMEM_LIMIT = 48 * 1024 * 1024
PROJ_TN = 1024
CV_W_INTER, CV_FLOOR, CV_W_STATE, CV_DECAY = 0, HEADS_A, 2 * HEADS_A, 3 * HEADS_A

F32 = jnp.float32
BF16 = jnp.bfloat16

assert all(w // d == BLOCK_B for w, d in GROUPS_B)


def _dot(a, b):
    return jnp.dot(a, b, preferred_element_type=F32)


def _dot_nt(a, b):
    return lax.dot_general(a, b, (((1,), (1,)), ((), ())), preferred_element_type=F32)


def _sigmoid(x):
    return 0.5 * jnp.tanh(0.5 * x) + 0.5


def _layer_norm_rows(r, g, b):
    mu = jnp.mean(r, axis=-1, keepdims=True)
    d = r - mu
    var = jnp.mean(d * d, axis=-1, keepdims=True)
    return d * lax.rsqrt(var + LN_EPS) * g + b


def _store_stream(r, x_ref, perm_refs, slab_ref, slab4_ref):
    x_ref[...] = r
    if not perm_refs:
        return
    xb1_ref, xb4_ref, xb16_ref = perm_refs
    xb1_ref[...] = r.astype(BF16)
    tm = r.shape[0]
    q4, q16 = tm // RES_STEP, tm // (RES_STEP * RES_STEP)
    for s in range(N_SLABS):
        lanes = slice(s * LANES, (s + 1) * LANES)
        slab_ref[s] = r[:, lanes]
        for r4 in range(RES_STEP):
            part = slab_ref[s, pl.ds(r4, q4, stride=RES_STEP), :]
            slab4_ref[s, r4 * q4:(r4 + 1) * q4, :] = part
            xb4_ref[r4, :, lanes] = part.astype(BF16)
        for r4 in range(RES_STEP):
            for a in range(RES_STEP):
                xb16_ref[r4 + RES_STEP * a, :, lanes] = (
                    slab4_ref[s, pl.ds(r4 * q4 + a, q16, stride=RES_STEP), :].astype(BF16))


def _stream_out(tm, with_perm):
    per_seq = SEQ // tm
    shapes = [jax.ShapeDtypeStruct((TOKENS, D_MODEL), F32)]
    specs = [pl.BlockSpec((tm, D_MODEL), lambda i: (i, 0))]
    if with_perm:
        shapes.append(jax.ShapeDtypeStruct((TOKENS, D_MODEL), BF16))
        specs.append(pl.BlockSpec((tm, D_MODEL), lambda i: (i, 0)))
        for d in DILATIONS[1:]:
            shapes.append(jax.ShapeDtypeStruct((BATCH, d, SEQ // d, D_MODEL), BF16))
            specs.append(pl.BlockSpec((None, d, tm // d, D_MODEL),
                                      lambda i: (i // per_seq, 0, i % per_seq, 0)))
    return shapes, specs


def _proj_kernel(x_ref, w_ref, o_ref, wb_ref, *, out_scale):
    @pl.when(pl.program_id(1) == 0)
    def _():
        wb_ref[...] = w_ref[...].astype(BF16)

    acc = _dot(x_ref[...].astype(BF16), wb_ref[...])
    if out_scale is not None:
        acc = acc * out_scale
    o_ref[...] = acc.astype(o_ref.dtype)


def _proj(x, w, layer, col0, n, name, out_scale=None, tm=2048, tn=PROJ_TN):
    m, k = x.shape
    col_blk0 = col0 // tn
    return pl.pallas_call(
        functools.partial(_proj_kernel, out_scale=out_scale),
        grid=(n // tn, m // tm),
        in_specs=[pl.BlockSpec((tm, k), lambda j, i: (i, 0)),
                  pl.BlockSpec((None, k, tn), lambda j, i: (layer, 0, col_blk0 + j))],
        out_specs=pl.BlockSpec((tm, tn), lambda j, i: (i, j)),
        out_shape=jax.ShapeDtypeStruct((m, n), BF16),
        scratch_shapes=[pltpu.VMEM((k, tn), BF16)],
        compiler_params=pltpu.CompilerParams(dimension_semantics=("parallel", "arbitrary"),
                                             vmem_limit_bytes=VMEM_LIMIT),
        name=name,
    )(x, w)


def _pre_kernel(xm_ref, halo_ref, cw_ref, cb_ref, bdqk_ref, bdv_ref, wgc_ref, wgm_ref, bg_ref,
                q_ref, k_ref, v_ref, xc_ref, wi_ref, cv_ref, ext_ref, m_ref, gates_ref, *, n_tiles):
    tm = xm_ref.shape[0]
    i = pl.program_id(0)
    tile = jnp.minimum(i, n_tiles - 1)
    seq_start = (tile * tm) % SEQ == 0
    prev_seq_start = ((i - 1) * tm) % SEQ == 0

    @pl.when(i == 0)
    def _():
        gates_ref[...] = jnp.zeros_like(gates_ref)
        m_ref[...] = jnp.zeros_like(m_ref)

    m_start = jnp.where(prev_seq_start, 0.0, m_ref[0:1, :])
    m_ref[0:1, :] = _gate_weights(gates_ref[...], m_start, wi_ref, cv_ref)

    halo = jnp.where(seq_start, 0.0, halo_ref[...].astype(F32))
    for blk in range(N_LANE_BLOCKS_A):
        sl = slice(blk * LANES, (blk + 1) * LANES)
        ext_ref[blk, pl.ds(0, SUBLANES, stride=2), :] = halo[:, sl]
        ext_ref[blk, pl.ds(2 * SUBLANES, tm, stride=2), :] = xm_ref[:, sl].astype(F32)
        acc = cb_ref[:, sl]
        for j in range(CONV_K):
            off = 2 * (SUBLANES - (CONV_K - 1) + j)
            acc = acc + cw_ref[j:j + 1, sl] * ext_ref[blk, pl.ds(off, tm, stride=2), :]
        xc_ref[:, sl] = (acc * _sigmoid(acc)).astype(BF16)
        qk = _dot(xc_ref[:, sl], bdqk_ref[blk])
        q_ref[:, sl] = qk[:, :LANES].astype(BF16)
        k_ref[:, sl] = qk[:, LANES:].astype(BF16)
        v_ref[:, sl] = _dot(xm_ref[:, sl], bdv_ref[blk]).astype(BF16)

    gates_ref[...] = _dot(xc_ref[...], wgc_ref[...]) + _dot(xm_ref[...], wgm_ref[...]) + bg_ref[...]


def _gate_weights(gates, m_start, wi_ref, cv_ref):
    tm = gates.shape[0]
    L = CHUNK_A
    scale = HEAD_DIM_A ** -0.5
    log_sig = jnp.minimum(gates, 0.0) - jnp.log(1.0 + jnp.exp(-jnp.abs(gates)))
    log_f = pltpu.roll(log_sig, LANES - HEADS_A, 1)
    row = lax.broadcasted_iota(jnp.int32, (L, L), 0)
    col = lax.broadcasted_iota(jnp.int32, (L, L), 1)
    tri = col <= row
    tri_b = jnp.where(tri, 1.0, 0.0).astype(BF16)
    lane = lax.broadcasted_iota(jnp.int32, (L, LANES), 1)
    head_lane = lax.broadcasted_iota(jnp.int32, (1, LANES), 1) < HEADS_A
    chunks = [slice(c * L, (c + 1) * L) for c in range(tm // L)]

    parts = []
    for rows in chunks:
        li = gates[rows, :]
        lf = log_f[rows, :]
        lf_hi = lf.astype(BF16)
        lf_mid = (lf - lf_hi.astype(F32)).astype(BF16)
        lf_lo = (lf - lf_hi.astype(F32) - lf_mid.astype(F32)).astype(BF16)
        b = _dot(tri_b, lf_hi) + _dot(tri_b, lf_mid) + _dot(tri_b, lf_lo)
        g_tot = b[L - 1:L, :]
        li_rows = li.T[0:SUBLANES, :]
        b_rows = b.T[0:SUBLANES, :]
        row_max = jnp.zeros((L, LANES), F32)
        for h in range(HEADS_A):
            d_intra = jnp.where(tri, b[:, h:h + 1] - b_rows[h:h + 1, :] + li_rows[h:h + 1, :], -jnp.inf)
            wi_ref[h, rows, :] = d_intra
            row_max = jnp.where(lane == h, jnp.max(d_intra, axis=1, keepdims=True), row_max)
        lw = g_tot - b + li
        parts.append((b, g_tot, lw, jnp.max(lw, axis=0, keepdims=True), row_max))

    m_list = [m_start]
    for b, g_tot, lw, lw_max, row_max in parts:
        m_list.append(jnp.where(head_lane, jnp.maximum(g_tot + m_list[-1], lw_max), 0.0))

    for c, rows in enumerate(chunks):
        b, g_tot, lw, lw_max, row_max = parts[c]
        m_prev, m_new = m_list[c], m_list[c + 1]
        a_inter = b + m_prev
        m_t = jnp.maximum(a_inter, row_max)
        for h in range(HEADS_A):
            wi_ref[h, rows, :] = jnp.exp(wi_ref[h, rows, :] - m_t[:, h:h + 1]) * scale
        w_inter = jnp.exp(a_inter - m_t)
        floor = jnp.exp(-m_t)
        w_state = jnp.exp(lw - m_new) * scale
        decay = jnp.broadcast_to(jnp.exp(g_tot + m_prev - m_new), (L, LANES))
        cv_ref[rows, :] = jnp.where(
            lane < CV_FLOOR, w_inter,
            jnp.where(lane < CV_W_STATE, pltpu.roll(floor, CV_FLOOR, 1),
                      jnp.where(lane < CV_DECAY, pltpu.roll(w_state, CV_W_STATE, 1),
                                jnp.where(lane < CV_DECAY + HEADS_A, pltpu.roll(decay, CV_DECAY, 1),
                                          0.0))))
    return m_list[-1]


def _mlstm_pre(proj, layer, cw, cb, bdqk, bdv, wgc, wgm, bg, name, tm=512):
    n_rows = proj.shape[0]
    n_tiles = n_rows // tm
    act = jax.ShapeDtypeStruct((n_rows, INNER_A), BF16)
    cur = lambda i: jnp.minimum(i, n_tiles - 1)
    prev = lambda i: jnp.maximum(i - 1, 0)
    row_spec = pl.BlockSpec((tm, INNER_A), lambda i: (cur(i), 0))
    full = lambda shape: pl.BlockSpec((None,) + shape, lambda i: (layer,) + (0,) * len(shape))
    return pl.pallas_call(
        functools.partial(_pre_kernel, n_tiles=n_tiles),
        grid=(n_tiles + 1,),
        in_specs=[row_spec,
                  pl.BlockSpec((SUBLANES, INNER_A),
                               lambda i: (jnp.maximum(cur(i) * (tm // SUBLANES) - 1, 0), 0)),
                  full((CONV_K, INNER_A)), full((1, INNER_A)),
                  full((N_LANE_BLOCKS_A, LANES, 2 * LANES)), full((N_LANE_BLOCKS_A, LANES, LANES)),
                  full((INNER_A, LANES)), full((INNER_A, LANES)), full((1, LANES))],
        out_specs=[row_spec, row_spec, row_spec, row_spec,
                   pl.BlockSpec((HEADS_A, tm, CHUNK_A), lambda i: (0, prev(i), 0)),
                   pl.BlockSpec((tm, LANES), lambda i: (prev(i), 0))],
        out_shape=[act, act, act, act,
                   jax.ShapeDtypeStruct((HEADS_A, n_rows, CHUNK_A), F32),
                   jax.ShapeDtypeStruct((n_rows, LANES), F32)],
        scratch_shapes=[pltpu.VMEM((N_LANE_BLOCKS_A, 2 * (tm + SUBLANES), LANES), F32),
                        pltpu.VMEM((SUBLANES, LANES), F32), pltpu.VMEM((tm, LANES), F32)],
        compiler_params=pltpu.CompilerParams(dimension_semantics=("arbitrary",),
                                             vmem_limit_bytes=VMEM_LIMIT),
        name=name,
    )(proj, proj, cw, cb, bdqk, bdv, wgc, wgm, bg)


def _mlstm_kernel(q_ref, k_ref, v_ref, xc_ref, z_ref, o_ref, wi_ref, cv_ref, gn_ref, sk_ref, y_ref,
                  ct_ref, n_ref):
    @pl.when(pl.program_id(1) == 0)
    def _():
        ct_ref[...] = jnp.zeros_like(ct_ref)
        n_ref[...] = jnp.zeros_like(n_ref)

    for c in range(q_ref.shape[0] // CHUNK_A):
        r = slice(c * CHUNK_A, (c + 1) * CHUNK_A)
        for h in range(HEADS_A):
            hs = slice(h * HEAD_DIM_A, (h + 1) * HEAD_DIM_A)
            qb = q_ref[r, hs]
            kb = k_ref[r, hs]
            vb = v_ref[r, hs]
            w_inter = cv_ref[r, CV_W_INTER + h:CV_W_INTER + h + 1]
            floor = cv_ref[r, CV_FLOOR + h:CV_FLOOR + h + 1]
            w_state = cv_ref[r, CV_W_STATE + h:CV_W_STATE + h + 1]
            decay = cv_ref[c * CHUNK_A:c * CHUNK_A + 1, CV_DECAY + h:CV_DECAY + h + 1]

            sc = _dot_nt(qb, kb) * wi_ref[h, r, :]
            num = w_inter * _dot(qb, ct_ref[h].astype(BF16)) + _dot(sc.astype(BF16), vb)
            n_rows = jnp.broadcast_to(n_ref[h].astype(BF16), (LANES, HEAD_DIM_A))
            den = w_inter * _dot_nt(qb, n_rows)[:, 0:1] + jnp.sum(sc, axis=1, keepdims=True)
            h_tilde = num * (1.0 / jnp.maximum(jnp.abs(den), floor))

            kw = kb.astype(F32) * w_state
            ct_ref[h] = decay * ct_ref[h] + _dot(kw.T.astype(BF16), vb)
            n_ref[h] = decay * n_ref[h] + jnp.sum(kw, axis=0, keepdims=True)

            hg = h_tilde * _sigmoid(o_ref[r, hs].astype(F32))
            mu = jnp.mean(hg, axis=1, keepdims=True)
            dlt = hg - mu
            var = jnp.mean(dlt * dlt, axis=1, keepdims=True)
            hn = dlt * lax.rsqrt(var + LN_EPS) * gn_ref[:, hs]
            zz = z_ref[r, hs].astype(F32)
            y = (hn + sk_ref[:, hs] * xc_ref[r, hs].astype(F32)) * (zz * _sigmoid(zz))
            y_ref[r, hs] = y.astype(BF16)


def _mlstm_core(q, k, v, xc, proj, wi, cv, layer, gn, skip, name, chunks_per_step=2):
    L = chunks_per_step * CHUNK_A
    dh = HEAD_DIM_A
    steps = SEQ // L
    row = lambda b, c: b * steps + c
    act_spec = pl.BlockSpec((L, INNER_A), lambda b, c: (row(b, c), 0))
    vec_spec = pl.BlockSpec((None, 1, INNER_A), lambda b, c: (layer, 0, 0))
    return pl.pallas_call(
        _mlstm_kernel,
        grid=(BATCH, steps),
        in_specs=[act_spec, act_spec, act_spec, act_spec,
                  pl.BlockSpec((L, INNER_A), lambda b, c: (row(b, c), 1)),
                  pl.BlockSpec((L, INNER_A), lambda b, c: (row(b, c), 2)),
                  pl.BlockSpec((HEADS_A, L, CHUNK_A), lambda b, c: (0, row(b, c), 0)),
                  pl.BlockSpec((L, LANES), lambda b, c: (row(b, c), 0)),
                  vec_spec, vec_spec],
        out_specs=act_spec,
        out_shape=jax.ShapeDtypeStruct((TOKENS, INNER_A), BF16),
        scratch_shapes=[pltpu.VMEM((HEADS_A, dh, dh), F32), pltpu.VMEM((HEADS_A, 1, dh), F32)],
        compiler_params=pltpu.CompilerParams(
            dimension_semantics=("parallel", "arbitrary"), vmem_limit_bytes=VMEM_LIMIT),
        name=name,
    )(q, k, v, xc, proj, proj, wi, cv, gn, skip)


def _out_ln_kernel(y_ref, w_ref, x_ref, g_ref, b_ref, *rest, with_perm):
    n_out = 1 + (N_GROUPS_B if with_perm else 0)
    outs, scratch = rest[:n_out], rest[n_out:]
    wb_ref = scratch[0]

    @pl.when(pl.program_id(0) == 0)
    def _():
        wb_ref[...] = w_ref[...].astype(BF16)

    r = ALPHA * x_ref[...] + _dot(y_ref[...], wb_ref[...])
    slabs = scratch[1:] if with_perm else (None, None)
    _store_stream(_layer_norm_rows(r, g_ref[...], b_ref[...]), outs[0], outs[1:], *slabs)


def _out_ln(y, w, layer, x, g, b, ln_layer, name, with_perm, tm=512):
    m, k = y.shape
    n = w.shape[2]
    out_shape, out_specs = _stream_out(tm, with_perm)
    return pl.pallas_call(
        functools.partial(_out_ln_kernel, with_perm=with_perm),
        grid=(m // tm,),
        in_specs=[pl.BlockSpec((tm, k), lambda i: (i, 0)),
                  pl.BlockSpec((None, k, n), lambda i: (layer, 0, 0)),
                  pl.BlockSpec((tm, n), lambda i: (i, 0)),
                  pl.BlockSpec((None, 1, n), lambda i: (ln_layer, 0, 0)),
                  pl.BlockSpec((None, 1, n), lambda i: (ln_layer, 0, 0))],
        out_specs=out_specs,
        out_shape=out_shape,
        scratch_shapes=([pltpu.VMEM((k, n), BF16)]
                        + ([pltpu.VMEM((N_SLABS, tm, LANES), F32)] * 2 if with_perm else [])),
        compiler_params=pltpu.CompilerParams(dimension_semantics=("arbitrary",),
                                             vmem_limit_bytes=VMEM_LIMIT),
        name=name,
    )(y, w, x, g, b)


def _attn_kernel(q_ref, kvp_ref, kvc_ref, o_ref, lse_ref, *, dilation, blocks_per_seq):
    n = BLOCK_B
    n_blocks = q_ref.shape[0] // n
    qi = lax.broadcasted_iota(jnp.int32, (n, n), 0)
    kj = lax.broadcasted_iota(jnp.int32, (n, n), 1)
    lower = kj <= qi
    diag = kj == qi
    dist = jnp.bitwise_and(qi - kj, n - 1).astype(F32)
    heads = range(HEADS_B)
    ksl = [slice(h * HEAD_DIM_B, (h + 1) * HEAD_DIM_B) for h in heads]
    vsl = [slice(INNER_B + h * HEAD_DIM_B, INNER_B + (h + 1) * HEAD_DIM_B) for h in heads]
    slopes = [2.0 ** (-8.0 * (h + 1.0) / HEADS_B) * dilation for h in heads]

    for blk in range(n_blocks):
        rows = slice(blk * n, (blk + 1) * n)
        prev_ref, prev_rows = ((kvp_ref, slice(0, n)) if blk == 0
                               else (kvc_ref, slice((blk - 1) * n, blk * n)))
        first = (pl.program_id(0) * n_blocks + blk) % blocks_per_seq == 0
        prev_bias = jnp.where(first, -jnp.inf, 0.0)

        both_rows = slice((blk - 1) * n, (blk + 1) * n)
        scores = []
        for h in heads:
            qh = q_ref[rows, ksl[h]]
            if blk == 0:
                s_prev = _dot_nt(qh, prev_ref[prev_rows, ksl[h]])
                s_cur = _dot_nt(qh, kvc_ref[rows, ksl[h]])
            else:
                s_both = _dot_nt(qh, kvc_ref[both_rows, ksl[h]])
                s_prev, s_cur = s_both[:, :n], s_both[:, n:]
            s_prev = s_prev + prev_bias
            s = jnp.where(lower, s_cur, s_prev) - slopes[h] * dist
            s_diag = (jnp.max(jnp.where(diag, s_prev, -jnp.inf), axis=-1, keepdims=True)
                      - slopes[h] * float(n))
            m = jnp.maximum(jnp.max(s, axis=-1, keepdims=True), s_diag)
            scores.append((s, s_diag, m))
        probs = []
        for h in heads:
            s, s_diag, m = scores[h]
            e = jnp.exp(s - m)
            e_diag = jnp.exp(s_diag - m)
            den = jnp.sum(e, axis=-1, keepdims=True) + e_diag
            p_prev = jnp.where(lower, jnp.where(diag, e_diag, 0.0), e)
            probs.append((jnp.where(lower, e, 0.0).astype(BF16), p_prev.astype(BF16), den))
        stats = jnp.ones((n, LANES), F32)
        for h in heads:
            p_cur, p_prev, den = probs[h]
            if blk == 0:
                o = _dot(p_cur, kvc_ref[rows, vsl[h]]) + _dot(p_prev, prev_ref[prev_rows, vsl[h]])
            else:
                o = _dot(jnp.concatenate([p_prev, p_cur], axis=1), kvc_ref[both_rows, vsl[h]])
            o_ref[rows, ksl[h]] = o.astype(BF16)
            stats = jnp.where(kj == h, den, jnp.where(kj == HEADS_B + h, scores[h][2], stats))
        lse_ref[rows, :] = stats


def _attn_group(q, kv, group, name, blocks_per_step=8):
    dilation = DILATIONS[group]
    blocks_per_seq = SEQ // dilation // BLOCK_B
    tq = blocks_per_step * BLOCK_B
    return pl.pallas_call(
        functools.partial(_attn_kernel, dilation=float(dilation), blocks_per_seq=blocks_per_seq),
        grid=(TOKENS // tq,),
        in_specs=[pl.BlockSpec((tq, INNER_B), lambda n: (n, 0)),
                  pl.BlockSpec((BLOCK_B, 2 * INNER_B),
                               lambda n: (jnp.maximum(n * blocks_per_step - 1, 0), 0)),
                  pl.BlockSpec((tq, 2 * INNER_B), lambda n: (n, 0))],
        out_specs=[pl.BlockSpec((tq, INNER_B), lambda n: (n, 0)),
                   pl.BlockSpec((tq, LANES), lambda n: (n, 0))],
        out_shape=[jax.ShapeDtypeStruct((TOKENS, INNER_B), BF16),
                   jax.ShapeDtypeStruct((TOKENS, LANES), F32)],
        compiler_params=pltpu.CompilerParams(dimension_semantics=("arbitrary",),
                                             vmem_limit_bytes=VMEM_LIMIT),
        name=name,
    )(q, kv, kv)


def _merge_out_ln_kernel(o0_ref, o1_ref, o2_ref, l0_ref, l1_ref, l2_ref, z_ref, w_ref, x_ref,
                         g_ref, b_ref, *rest, with_perm):
    n_out = 1 + (N_GROUPS_B if with_perm else 0)
    outs = rest[:n_out]
    y_ref, os1_ref, os2_ref, tmp_ref, ls1_ref, ls2_ref, ltmp_ref, wb_ref = rest[n_out:]
    tm = x_ref.shape[0]

    @pl.when(pl.program_id(0) == 0)
    def _():
        wb_ref[...] = w_ref[...].astype(BF16)

    q4, q16 = tm // RES_STEP, tm // (RES_STEP * RES_STEP)
    for r4 in range(RES_STEP):
        ls1_ref[pl.ds(r4, q4, stride=RES_STEP), :] = l1_ref[r4]
        for a in range(RES_STEP):
            ltmp_ref[pl.ds(r4 * q4 + a, q16, stride=RES_STEP), :] = l2_ref[r4 + RES_STEP * a]
        ls2_ref[pl.ds(r4, q4, stride=RES_STEP), :] = ltmp_ref[r4 * q4:(r4 + 1) * q4, :]
    for s in range(N_SLABS):
        lanes = slice(s * LANES, (s + 1) * LANES)
        for r4 in range(RES_STEP):
            os1_ref[s, pl.ds(r4, q4, stride=RES_STEP), :] = o1_ref[r4, :, lanes].astype(F32)
            for a in range(RES_STEP):
                tmp_ref[s, pl.ds(r4 * q4 + a, q16, stride=RES_STEP), :] = (
                    o2_ref[r4 + RES_STEP * a, :, lanes].astype(F32))
            os2_ref[s, pl.ds(r4, q4, stride=RES_STEP), :] = tmp_ref[s, r4 * q4:(r4 + 1) * q4, :]
    stats = (l0_ref[...], ls1_ref[...], ls2_ref[...])
    maxes = [pltpu.roll(t, LANES - HEADS_B, 1) for t in stats]
    head_lane = lax.broadcasted_iota(jnp.int32, (1, LANES), 1) < HEADS_B
    lses = [m + jnp.log(jnp.where(head_lane, t, 1.0)) for m, t in zip(maxes, stats)]
    mx = jnp.maximum(jnp.maximum(lses[0], lses[1]), lses[2])
    inv = 1.0 / (jnp.exp(lses[0] - mx) + jnp.exp(lses[1] - mx) + jnp.exp(lses[2] - mx))
    w0, w1, w2 = [jnp.exp(m - mx) * inv for m in maxes]
    for h in range(HEADS_B):
        sl = slice(h * HEAD_DIM_B, (h + 1) * HEAD_DIM_B)
        o = (w0[:, h:h + 1] * o0_ref[:, sl].astype(F32) + w1[:, h:h + 1] * os1_ref[h]
             + w2[:, h:h + 1] * os2_ref[h])
        zz = z_ref[:, sl].astype(F32)
        y_ref[:, sl] = (o * (zz * _sigmoid(zz))).astype(BF16)
    r = ALPHA * x_ref[...] + _dot(y_ref[...], wb_ref[...])
    _store_stream(_layer_norm_rows(r, g_ref[...], b_ref[...]), outs[0], outs[1:], os1_ref, os2_ref)


def _merge_out_ln(os_, lses, z, w, layer, x, g, b, ln_layer, name, with_perm, tm=512):
    m = x.shape[0]
    n = D_MODEL
    per_seq = SEQ // tm
    tok = lambda width: pl.BlockSpec((tm, width), lambda i: (i, 0))
    res = lambda d, width: pl.BlockSpec((None, d, tm // d, width),
                                        lambda i: (i // per_seq, 0, i % per_seq, 0))
    vec = pl.BlockSpec((None, 1, n), lambda i: (ln_layer, 0, 0))
    d1, d2 = DILATIONS[1], DILATIONS[2]
    out_shape, out_specs = _stream_out(tm, with_perm)
    return pl.pallas_call(
        functools.partial(_merge_out_ln_kernel, with_perm=with_perm),
        grid=(m // tm,),
        in_specs=[tok(INNER_B), res(d1, INNER_B), res(d2, INNER_B),
                  tok(LANES), res(d1, LANES), res(d2, LANES),
                  tok(INNER_B),
                  pl.BlockSpec((None, INNER_B, n), lambda i: (layer, 0, 0)),
                  tok(n), vec, vec],
        out_specs=out_specs,
        out_shape=out_shape,
        scratch_shapes=[pltpu.VMEM((tm, INNER_B), BF16),
                        pltpu.VMEM((N_SLABS, tm, LANES), F32), pltpu.VMEM((N_SLABS, tm, LANES), F32),
                        pltpu.VMEM((N_SLABS, tm, LANES), F32),
                        pltpu.VMEM((tm, LANES), F32), pltpu.VMEM((tm, LANES), F32),
                        pltpu.VMEM((tm, LANES), F32), pltpu.VMEM((INNER_B, n), BF16)],
        compiler_params=pltpu.CompilerParams(dimension_semantics=("arbitrary",),
                                             vmem_limit_bytes=VMEM_LIMIT),
        name=name,
    )(os_[0], os_[1].reshape(BATCH, d1, SEQ // d1, INNER_B), os_[2].reshape(BATCH, d2, SEQ // d2, INNER_B),
      lses[0], lses[1].reshape(BATCH, d1, SEQ // d1, LANES), lses[2].reshape(BATCH, d2, SEQ // d2, LANES),
      z, w, x, g, b)


def _block_diag_lane_blocks(w):
    per = LANES // QKV_BLOCK
    w4 = w.reshape(-1, per, QKV_BLOCK, QKV_BLOCK)
    eye = jnp.eye(per, dtype=w.dtype)
    dense = jnp.einsum('bmij,mp->bmjpi', w4, eye)
    return dense.reshape(-1, LANES, LANES)


def _fold_gate_weights(wq, wk, wv, w_if):
    nblk = INNER_A // QKV_BLOCK
    wif = w_if.reshape(3, nblk, QKV_BLOCK, 2 * HEADS_A)
    hp = lax.Precision.HIGHEST
    wgc = (jnp.einsum('nij,nio->njo', wq, wif[0], precision=hp)
           + jnp.einsum('nij,nio->njo', wk, wif[1], precision=hp)).reshape(INNER_A, 2 * HEADS_A)
    wgm = jnp.einsum('nij,nio->njo', wv, wif[2], precision=hp).reshape(INNER_A, 2 * HEADS_A)
    pad = ((0, 0), (0, LANES - 2 * HEADS_A))
    return jnp.pad(wgc, pad).astype(BF16), jnp.pad(wgm, pad).astype(BF16)


def kernel(x, ln_g, ln_b, a_w_in, a_conv_w, a_conv_b, a_wq, a_wk, a_wv, a_w_if, a_b_if, a_gn_g, a_skip,
           a_w_out, b_w_kv, b_w_in, b_w_out):
    xs = x.reshape(TOKENS, D_MODEL)
    x_in = xs
    block_diag = jax.vmap(_block_diag_lane_blocks)
    bdqk = jnp.concatenate([block_diag(a_wq), block_diag(a_wk)], axis=-1).astype(BF16)
    bdv = block_diag(a_wv).astype(BF16)
    wgc, wgm = jax.vmap(_fold_gate_weights)(a_wq, a_wk, a_wv, a_w_if)
    bg = jnp.pad(a_b_if[:, None, :], ((0, 0), (0, 0), (0, LANES - 2 * HEADS_A)))
    conv_b, gn, skip = a_conv_b[:, None, :], a_gn_g[:, None, :], a_skip[:, None, :]
    ln_g3, ln_b3 = ln_g[:, None, :], ln_b[:, None, :]
    for layer in range(N_A_LAYERS):
        proj = _proj(x_in, a_w_in, layer, 0, 3 * INNER_A, f"a{layer}_proj")
        q, k, v, xc, wi, cv = _mlstm_pre(proj, layer, a_conv_w, conv_b, bdqk, bdv, wgc, wgm, bg,
                                         f"a{layer}_pre")
        y = _mlstm_core(q, k, v, xc, proj, wi, cv, layer, gn, skip, f"a{layer}_mlstm")
        outs = _out_ln(y, a_w_out, layer, xs, ln_g3, ln_b3, layer, f"a{layer}_out_ln",
                       with_perm=layer == N_A_LAYERS - 1)
        xs = outs[0]
        x_in = xs

    perms = [p.reshape(TOKENS, D_MODEL) for p in outs[1:]]
    kvs = [_proj(perms[g], b_w_kv[None], 0, 2 * g * INNER_B, 2 * INNER_B, f"kv_proj{g}")
           for g in range(N_GROUPS_B)]
    q_scale = HEAD_DIM_B ** -0.5
    for lb in range(N_B_LAYERS):
        layer = N_A_LAYERS + lb
        qs = [_proj(perms[g], b_w_in, lb, g * INNER_B, INNER_B, f"b{lb}_proj{g}", out_scale=q_scale)
              for g in range(N_GROUPS_B)]
        z = _proj(perms[0], b_w_in, lb, N_GROUPS_B * INNER_B, INNER_B, f"b{lb}_projz")
        os_, lses = [], []
        for g in range(N_GROUPS_B):
            o, lse = _attn_group(qs[g], kvs[g], g, f"b{lb}_attn{g}")
            os_.append(o)
            lses.append(lse)
        outs = _merge_out_ln(os_, lses, z, b_w_out, lb, xs, ln_g3, ln_b3, layer,
                             f"b{lb}_merge_out_ln", with_perm=lb < N_B_LAYERS - 1)
        xs = outs[0]
        perms = [p.reshape(TOKENS, D_MODEL) for p in outs[1:]]
    return xs.reshape(BATCH, SEQ, D_MODEL)
```

```python
import functools

import jax
import jax.numpy as jnp
from jax import lax
from jax.experimental import pallas as pl
from jax.experimental.pallas import tpu as pltpu

D_MODEL = 1024
BATCH = 4
SEQ = 4096
DEPTH = 4
N_A_LAYERS = DEPTH // 2
N_B_LAYERS = DEPTH - N_A_LAYERS
INNER_A = 2 * D_MODEL
HEADS_A = 4
HEAD_DIM_A = INNER_A // HEADS_A
QKV_BLOCK = 4
CONV_K = 4
HEAD_DIM_B = 128
HEADS_B = D_MODEL // HEAD_DIM_B
INNER_B = HEADS_B * HEAD_DIM_B
GROUPS_B = ((128, 1), (512, 4), (2048, 16))
N_GROUPS_B = len(GROUPS_B)
DILATIONS = tuple(d for _, d in GROUPS_B)
BLOCK_B = 128
RES_STEP = 4
assert DILATIONS == (1, RES_STEP, RES_STEP * RES_STEP)
ALPHA = (2 * DEPTH) ** 0.25
LN_EPS = 1e-5

TOKENS = BATCH * SEQ
LANES = 128
SUBLANES = 8
N_SLABS = D_MODEL // LANES
CHUNK_A = 256
N_LANE_BLOCKS_A = INNER_A // LANES
VMEM_LIMIT = 48 * 1024 * 1024
PROJ_TN = 1024
PROJ_TM_BYTES = 8 * 1024 * 1024
PROJ_VMEM_LIMIT = 56 * 1024 * 1024
CV_W_INTER, CV_FLOOR, CV_W_STATE, CV_DECAY = 0, HEADS_A, 2 * HEADS_A, 3 * HEADS_A

F32 = jnp.float32
BF16 = jnp.bfloat16

assert all(w // d == BLOCK_B for w, d in GROUPS_B)


def _dot(a, b):
    return jnp.dot(a, b, preferred_element_type=F32)


def _dot_nt(a, b):
    return lax.dot_general(a, b, (((1,), (1,)), ((), ())), preferred_element_type=F32)


def _sigmoid(x):
    return 0.5 * jnp.tanh(0.5 * x) + 0.5


def _layer_norm_rows(r, g, b):
    mu = jnp.mean(r, axis=-1, keepdims=True)
    d = r - mu
    var = jnp.mean(d * d, axis=-1, keepdims=True)
    return d * lax.rsqrt(var + LN_EPS) * g + b


def _store_stream(r, x_ref, perm_refs, slab_ref, slab4_ref):
    x_ref[...] = r
    if not perm_refs:
        return
    xb1_ref, xb4_ref, xb16_ref = perm_refs
    xb1_ref[...] = r.astype(BF16)
    tm = r.shape[0]
    q4, q16 = tm // RES_STEP, tm // (RES_STEP * RES_STEP)
    for s in range(N_SLABS):
        lanes = slice(s * LANES, (s + 1) * LANES)
        slab_ref[s] = r[:, lanes]
        for r4 in range(RES_STEP):
            part = slab_ref[s, pl.ds(r4, q4, stride=RES_STEP), :]
            slab4_ref[s, r4 * q4:(r4 + 1) * q4, :] = part
            xb4_ref[r4, :, lanes] = part.astype(BF16)
        for r4 in range(RES_STEP):
            for a in range(RES_STEP):
                xb16_ref[r4 + RES_STEP * a, :, lanes] = (
                    slab4_ref[s, pl.ds(r4 * q4 + a, q16, stride=RES_STEP), :].astype(BF16))


def _stream_out(tm, with_perm):
    per_seq = SEQ // tm
    shapes = [jax.ShapeDtypeStruct((TOKENS, D_MODEL), F32)]
    specs = [pl.BlockSpec((tm, D_MODEL), lambda i: (i, 0))]
    if with_perm:
        shapes.append(jax.ShapeDtypeStruct((TOKENS, D_MODEL), BF16))
        specs.append(pl.BlockSpec((tm, D_MODEL), lambda i: (i, 0)))
        for d in DILATIONS[1:]:
            shapes.append(jax.ShapeDtypeStruct((BATCH, d, SEQ // d, D_MODEL), BF16))
            specs.append(pl.BlockSpec((None, d, tm // d, D_MODEL),
                                      lambda i: (i // per_seq, 0, i % per_seq, 0)))
    return shapes, specs


def _proj_kernel(x_ref, w_ref, o_ref, wb_ref, *, out_scale):
    @pl.when(pl.program_id(1) == 0)
    def _():
        wb_ref[...] = w_ref[...].astype(BF16)

    acc = _dot(x_ref[...].astype(BF16), wb_ref[...])
    if out_scale is not None:
        acc = acc * out_scale
    o_ref[...] = acc.astype(o_ref.dtype)


def _proj(x, w, layer, col0, n, name, out_scale=None, tn=PROJ_TN):
    m, k = x.shape
    tm = PROJ_TM_BYTES // (k * x.dtype.itemsize)
    col_blk0 = col0 // tn
    w_mode = pl.Buffered(1) if tm * tn * 4 > PROJ_TM_BYTES else None
    return pl.pallas_call(
        functools.partial(_proj_kernel, out_scale=out_scale),
        grid=(n // tn, m // tm),
        in_specs=[pl.BlockSpec((tm, k), lambda j, i: (i, 0)),
                  pl.BlockSpec((None, k, tn), lambda j, i: (layer, 0, col_blk0 + j),
                               pipeline_mode=w_mode)],
        out_specs=pl.BlockSpec((tm, tn), lambda j, i: (i, j)),
        out_shape=jax.ShapeDtypeStruct((m, n), BF16),
        scratch_shapes=[pltpu.VMEM((k, tn), BF16)],
        compiler_params=pltpu.CompilerParams(dimension_semantics=("parallel", "arbitrary"),
                                             vmem_limit_bytes=PROJ_VMEM_LIMIT),
        name=name,
    )(x, w)


def _pre_kernel(xm_ref, halo_ref, cw_ref, cb_ref, bdqk_ref, bdv_ref, wgc_ref, wgm_ref, bg_ref,
                q_ref, k_ref, v_ref, xc_ref, wi_ref, cv_ref, ext_ref, m_ref, gates_ref, *, n_tiles):
    tm = xm_ref.shape[0]
    i = pl.program_id(0)
    tile = jnp.minimum(i, n_tiles - 1)
    seq_start = (tile * tm) % SEQ == 0
    prev_seq_start = ((i - 1) * tm) % SEQ == 0

    @pl.when(i == 0)
    def _():
        gates_ref[...] = jnp.zeros_like(gates_ref)
        m_ref[...] = jnp.zeros_like(m_ref)

    m_start = jnp.where(prev_seq_start, 0.0, m_ref[0:1, :])
    m_ref[0:1, :] = _gate_weights(gates_ref[...], m_start, wi_ref, cv_ref)

    halo = jnp.where(seq_start, 0.0, halo_ref[...].astype(F32))
    for blk in range(N_LANE_BLOCKS_A):
        sl = slice(blk * LANES, (blk + 1) * LANES)
        ext_ref[blk, pl.ds(0, SUBLANES, stride=2), :] = halo[:, sl]
        ext_ref[blk, pl.ds(2 * SUBLANES, tm, stride=2), :] = xm_ref[:, sl].astype(F32)
        acc = cb_ref[:, sl]
        for j in range(CONV_K):
            off = 2 * (SUBLANES - (CONV_K - 1) + j)
            acc = acc + cw_ref[j:j + 1, sl] * ext_ref[blk, pl.ds(off, tm, stride=2), :]
        xc_ref[:, sl] = (acc * _sigmoid(acc)).astype(BF16)
        qk = _dot(xc_ref[:, sl], bdqk_ref[blk])
        q_ref[:, sl] = qk[:, :LANES].astype(BF16)
        k_ref[:, sl] = qk[:, LANES:].astype(BF16)
        v_ref[:, sl] = _dot(xm_ref[:, sl], bdv_ref[blk]).astype(BF16)

    gates_ref[...] = _dot(xc_ref[...], wgc_ref[...]) + _dot(xm_ref[...], wgm_ref[...]) + bg_ref[...]


def _gate_weights(gates, m_start, wi_ref, cv_ref):
    tm = gates.shape[0]
    L = CHUNK_A
    scale = HEAD_DIM_A ** -0.5
    log_sig = jnp.minimum(gates, 0.0) - jnp.log(1.0 + jnp.exp(-jnp.abs(gates)))
    log_f = pltpu.roll(log_sig, LANES - HEADS_A, 1)
    row = lax.broadcasted_iota(jnp.int32, (L, L), 0)
    col = lax.broadcasted_iota(jnp.int32, (L, L), 1)
    tri = col <= row
    tri_b = jnp.where(tri, 1.0, 0.0).astype(BF16)
    lane = lax.broadcasted_iota(jnp.int32, (L, LANES), 1)
    head_lane = lax.broadcasted_iota(jnp.int32, (1, LANES), 1) < HEADS_A
    chunks = [slice(c * L, (c + 1) * L) for c in range(tm // L)]

    parts = []
    for rows in chunks:
        li = gates[rows, :]
        lf = log_f[rows, :]
        lf_hi = lf.astype(BF16)
        lf_mid = (lf - lf_hi.astype(F32)).astype(BF16)
        lf_lo = (lf - lf_hi.astype(F32) - lf_mid.astype(F32)).astype(BF16)
        b = _dot(tri_b, lf_hi) + _dot(tri_b, lf_mid) + _dot(tri_b, lf_lo)
        g_tot = b[L - 1:L, :]
        li_rows = li.T[0:SUBLANES, :]
        b_rows = b.T[0:SUBLANES, :]
        row_max = jnp.zeros((L, LANES), F32)
        for h in range(HEADS_A):
            d_intra = jnp.where(tri, b[:, h:h + 1] - b_rows[h:h + 1, :] + li_rows[h:h + 1, :], -jnp.inf)
            wi_ref[h, rows, :] = d_intra
            row_max = jnp.where(lane == h, jnp.max(d_intra, axis=1, keepdims=True), row_max)
        lw = g_tot - b + li
        parts.append((b, g_tot, lw, jnp.max(lw, axis=0, keepdims=True), row_max))

    m_list = [m_start]
    for b, g_tot, lw, lw_max, row_max in parts:
        m_list.append(jnp.where(head_lane, jnp.maximum(g_tot + m_list[-1], lw_max), 0.0))

    for c, rows in enumerate(chunks):
        b, g_tot, lw, lw_max, row_max = parts[c]
        m_prev, m_new = m_list[c], m_list[c + 1]
        a_inter = b + m_prev
        m_t = jnp.maximum(a_inter, row_max)
        for h in range(HEADS_A):
            wi_ref[h, rows, :] = jnp.exp(wi_ref[h, rows, :] - m_t[:, h:h + 1]) * scale
        w_inter = jnp.exp(a_inter - m_t)
        floor = jnp.exp(-m_t)
        w_state = jnp.exp(lw - m_new) * scale
        decay = jnp.broadcast_to(jnp.exp(g_tot + m_prev - m_new), (L, LANES))
        cv_ref[rows, :] = jnp.where(
            lane < CV_FLOOR, w_inter,
            jnp.where(lane < CV_W_STATE, pltpu.roll(floor, CV_FLOOR, 1),
                      jnp.where(lane < CV_DECAY, pltpu.roll(w_state, CV_W_STATE, 1),
                                jnp.where(lane < CV_DECAY + HEADS_A, pltpu.roll(decay, CV_DECAY, 1),
                                          0.0))))
    return m_list[-1]


def _mlstm_pre(proj, layer, cw, cb, bdqk, bdv, wgc, wgm, bg, name, tm=512):
    n_rows = proj.shape[0]
    n_tiles = n_rows // tm
    act = jax.ShapeDtypeStruct((n_rows, INNER_A), BF16)
    cur = lambda i: jnp.minimum(i, n_tiles - 1)
    prev = lambda i: jnp.maximum(i - 1, 0)
    row_spec = pl.BlockSpec((tm, INNER_A), lambda i: (cur(i), 0))
    full = lambda shape: pl.BlockSpec((None,) + shape, lambda i: (layer,) + (0,) * len(shape))
    return pl.pallas_call(
        functools.partial(_pre_kernel, n_tiles=n_tiles),
        grid=(n_tiles + 1,),
        in_specs=[row_spec,
                  pl.BlockSpec((SUBLANES, INNER_A),
                               lambda i: (jnp.maximum(cur(i) * (tm // SUBLANES) - 1, 0), 0)),
                  full((CONV_K, INNER_A)), full((1, INNER_A)),
                  full((N_LANE_BLOCKS_A, LANES, 2 * LANES)), full((N_LANE_BLOCKS_A, LANES, LANES)),
                  full((INNER_A, LANES)), full((INNER_A, LANES)), full((1, LANES))],
        out_specs=[row_spec, row_spec, row_spec, row_spec,
                   pl.BlockSpec((HEADS_A, tm, CHUNK_A), lambda i: (0, prev(i), 0)),
                   pl.BlockSpec((tm, LANES), lambda i: (prev(i), 0))],
        out_shape=[act, act, act, act,
                   jax.ShapeDtypeStruct((HEADS_A, n_rows, CHUNK_A), F32),
                   jax.ShapeDtypeStruct((n_rows, LANES), F32)],
        scratch_shapes=[pltpu.VMEM((N_LANE_BLOCKS_A, 2 * (tm + SUBLANES), LANES), F32),
                        pltpu.VMEM((SUBLANES, LANES), F32), pltpu.VMEM((tm, LANES), F32)],
        compiler_params=pltpu.CompilerParams(dimension_semantics=("arbitrary",),
                                             vmem_limit_bytes=VMEM_LIMIT),
        name=name,
    )(proj, proj, cw, cb, bdqk, bdv, wgc, wgm, bg)


def _mlstm_kernel(q_ref, k_ref, v_ref, xc_ref, z_ref, o_ref, wi_ref, cv_ref, gn_ref, sk_ref, y_ref,
                  ct_ref, n_ref):
    @pl.when(pl.program_id(1) == 0)
    def _():
        ct_ref[...] = jnp.zeros_like(ct_ref)
        n_ref[...] = jnp.zeros_like(n_ref)

    for c in range(q_ref.shape[0] // CHUNK_A):
        r = slice(c * CHUNK_A, (c + 1) * CHUNK_A)
        for h in range(HEADS_A):
            hs = slice(h * HEAD_DIM_A, (h + 1) * HEAD_DIM_A)
            qb = q_ref[r, hs]
            kb = k_ref[r, hs]
            vb = v_ref[r, hs]
            w_inter = cv_ref[r, CV_W_INTER + h:CV_W_INTER + h + 1]
            floor = cv_ref[r, CV_FLOOR + h:CV_FLOOR + h + 1]
            w_state = cv_ref[r, CV_W_STATE + h:CV_W_STATE + h + 1]
            decay = cv_ref[c * CHUNK_A:c * CHUNK_A + 1, CV_DECAY + h:CV_DECAY + h + 1]

            sc = _dot_nt(qb, kb) * wi_ref[h, r, :]
            num = w_inter * _dot(qb, ct_ref[h].astype(BF16)) + _dot(sc.astype(BF16), vb)
            n_rows = jnp.broadcast_to(n_ref[h].astype(BF16), (LANES, HEAD_DIM_A))
            den = w_inter * _dot_nt(qb, n_rows)[:, 0:1] + jnp.sum(sc, axis=1, keepdims=True)
            h_tilde = num * (1.0 / jnp.maximum(jnp.abs(den), floor))

            kw = kb.astype(F32) * w_state
            ct_ref[h] = decay * ct_ref[h] + _dot(kw.T.astype(BF16), vb)
            n_ref[h] = decay * n_ref[h] + jnp.sum(kw, axis=0, keepdims=True)

            hg = h_tilde * _sigmoid(o_ref[r, hs].astype(F32))
            mu = jnp.mean(hg, axis=1, keepdims=True)
            dlt = hg - mu
            var = jnp.mean(dlt * dlt, axis=1, keepdims=True)
            hn = dlt * lax.rsqrt(var + LN_EPS) * gn_ref[:, hs]
            zz = z_ref[r, hs].astype(F32)
            y = (hn + sk_ref[:, hs] * xc_ref[r, hs].astype(F32)) * (zz * _sigmoid(zz))
            y_ref[r, hs] = y.astype(BF16)


def _mlstm_core(q, k, v, xc, proj, wi, cv, layer, gn, skip, name, chunks_per_step=2):
    L = chunks_per_step * CHUNK_A
    dh = HEAD_DIM_A
    steps = SEQ // L
    row = lambda b, c: b * steps + c
    act_spec = pl.BlockSpec((L, INNER_A), lambda b, c: (row(b, c), 0))
    vec_spec = pl.BlockSpec((None, 1, INNER_A), lambda b, c: (layer, 0, 0))
    return pl.pallas_call(
        _mlstm_kernel,
        grid=(BATCH, steps),
        in_specs=[act_spec, act_spec, act_spec, act_spec,
                  pl.BlockSpec((L, INNER_A), lambda b, c: (row(b, c), 1)),
                  pl.BlockSpec((L, INNER_A), lambda b, c: (row(b, c), 2)),
                  pl.BlockSpec((HEADS_A, L, CHUNK_A), lambda b, c: (0, row(b, c), 0)),
                  pl.BlockSpec((L, LANES), lambda b, c: (row(b, c), 0)),
                  vec_spec, vec_spec],
        out_specs=act_spec,
        out_shape=jax.ShapeDtypeStruct((TOKENS, INNER_A), BF16),
        scratch_shapes=[pltpu.VMEM((HEADS_A, dh, dh), F32), pltpu.VMEM((HEADS_A, 1, dh), F32)],
        compiler_params=pltpu.CompilerParams(
            dimension_semantics=("parallel", "arbitrary"), vmem_limit_bytes=VMEM_LIMIT),
        name=name,
    )(q, k, v, xc, proj, proj, wi, cv, gn, skip)


def _out_ln_kernel(y_ref, w_ref, x_ref, g_ref, b_ref, *rest, with_perm):
    n_out = 1 + (N_GROUPS_B if with_perm else 0)
    outs, scratch = rest[:n_out], rest[n_out:]
    wb_ref = scratch[0]

    @pl.when(pl.program_id(0) == 0)
    def _():
        wb_ref[...] = w_ref[...].astype(BF16)

    r = ALPHA * x_ref[...] + _dot(y_ref[...], wb_ref[...])
    slabs = scratch[1:] if with_perm else (None, None)
    _store_stream(_layer_norm_rows(r, g_ref[...], b_ref[...]), outs[0], outs[1:], *slabs)


def _out_ln(y, w, layer, x, g, b, ln_layer, name, with_perm, tm=512):
    m, k = y.shape
    n = w.shape[2]
    out_shape, out_specs = _stream_out(tm, with_perm)
    return pl.pallas_call(
        functools.partial(_out_ln_kernel, with_perm=with_perm),
        grid=(m // tm,),
        in_specs=[pl.BlockSpec((tm, k), lambda i: (i, 0)),
                  pl.BlockSpec((None, k, n), lambda i: (layer, 0, 0)),
                  pl.BlockSpec((tm, n), lambda i: (i, 0)),
                  pl.BlockSpec((None, 1, n), lambda i: (ln_layer, 0, 0)),
                  pl.BlockSpec((None, 1, n), lambda i: (ln_layer, 0, 0))],
        out_specs=out_specs,
        out_shape=out_shape,
        scratch_shapes=([pltpu.VMEM((k, n), BF16)]
                        + ([pltpu.VMEM((N_SLABS, tm, LANES), F32)] * 2 if with_perm else [])),
        compiler_params=pltpu.CompilerParams(dimension_semantics=("arbitrary",),
                                             vmem_limit_bytes=VMEM_LIMIT),
        name=name,
    )(y, w, x, g, b)


def _attn_kernel(q_ref, kvp_ref, kvc_ref, o_ref, lse_ref, *, dilation, blocks_per_seq):
    n = BLOCK_B
    n_blocks = q_ref.shape[0] // n
    qi = lax.broadcasted_iota(jnp.int32, (n, n), 0)
    kj = lax.broadcasted_iota(jnp.int32, (n, n), 1)
    lower = kj <= qi
    diag = kj == qi
    dist = jnp.bitwise_and(qi - kj, n - 1).astype(F32)
    heads = range(HEADS_B)
    ksl = [slice(h * HEAD_DIM_B, (h + 1) * HEAD_DIM_B) for h in heads]
    vsl = [slice(INNER_B + h * HEAD_DIM_B, INNER_B + (h + 1) * HEAD_DIM_B) for h in heads]
    slopes = [2.0 ** (-8.0 * (h + 1.0) / HEADS_B) * dilation for h in heads]

    for blk in range(n_blocks):
        rows = slice(blk * n, (blk + 1) * n)
        prev_ref, prev_rows = ((kvp_ref, slice(0, n)) if blk == 0
                               else (kvc_ref, slice((blk - 1) * n, blk * n)))
        first = (pl.program_id(0) * n_blocks + blk) % blocks_per_seq == 0
        prev_bias = jnp.where(first, -jnp.inf, 0.0)

        both_rows = slice((blk - 1) * n, (blk + 1) * n)
        scores = []
        for h in heads:
            qh = q_ref[rows, ksl[h]]
            if blk == 0:
                s_prev = _dot_nt(qh, prev_ref[prev_rows, ksl[h]])
                s_cur = _dot_nt(qh, kvc_ref[rows, ksl[h]])
            else:
                s_both = _dot_nt(qh, kvc_ref[both_rows, ksl[h]])
                s_prev, s_cur = s_both[:, :n], s_both[:, n:]
            s_prev = s_prev + prev_bias
            s = jnp.where(lower, s_cur, s_prev) - slopes[h] * dist
            s_diag = (jnp.max(jnp.where(diag, s_prev, -jnp.inf), axis=-1, keepdims=True)
                      - slopes[h] * float(n))
            m = jnp.maximum(jnp.max(s, axis=-1, keepdims=True), s_diag)
            scores.append((s, s_diag, m))
        probs = []
        for h in heads:
            s, s_diag, m = scores[h]
            e = jnp.exp(s - m)
            e_diag = jnp.exp(s_diag - m)
            den = jnp.sum(e, axis=-1, keepdims=True) + e_diag
            p_prev = jnp.where(lower, jnp.where(diag, e_diag, 0.0), e)
            probs.append((jnp.where(lower, e, 0.0).astype(BF16), p_prev.astype(BF16), den))
        stats = jnp.ones((n, LANES), F32)
        for h in heads:
            p_cur, p_prev, den = probs[h]
            if blk == 0:
                o = _dot(p_cur, kvc_ref[rows, vsl[h]]) + _dot(p_prev, prev_ref[prev_rows, vsl[h]])
            else:
                o = _dot(jnp.concatenate([p_prev, p_cur], axis=1), kvc_ref[both_rows, vsl[h]])
            o_ref[rows, ksl[h]] = o.astype(BF16)
            stats = jnp.where(kj == h, den, jnp.where(kj == HEADS_B + h, scores[h][2], stats))
        lse_ref[rows, :] = stats


def _attn_group(q, kv, group, name, blocks_per_step=8):
    dilation = DILATIONS[group]
    blocks_per_seq = SEQ // dilation // BLOCK_B
    tq = blocks_per_step * BLOCK_B
    return pl.pallas_call(
        functools.partial(_attn_kernel, dilation=float(dilation), blocks_per_seq=blocks_per_seq),
        grid=(TOKENS // tq,),
        in_specs=[pl.BlockSpec((tq, INNER_B), lambda n: (n, 0)),
                  pl.BlockSpec((BLOCK_B, 2 * INNER_B),
                               lambda n: (jnp.maximum(n * blocks_per_step - 1, 0), 0)),
                  pl.BlockSpec((tq, 2 * INNER_B), lambda n: (n, 0))],
        out_specs=[pl.BlockSpec((tq, INNER_B), lambda n: (n, 0)),
                   pl.BlockSpec((tq, LANES), lambda n: (n, 0))],
        out_shape=[jax.ShapeDtypeStruct((TOKENS, INNER_B), BF16),
                   jax.ShapeDtypeStruct((TOKENS, LANES), F32)],
        compiler_params=pltpu.CompilerParams(dimension_semantics=("arbitrary",),
                                             vmem_limit_bytes=VMEM_LIMIT),
        name=name,
    )(q, kv, kv)


def _merge_out_ln_kernel(o0_ref, o1_ref, o2_ref, l0_ref, l1_ref, l2_ref, z_ref, w_ref, x_ref,
                         g_ref, b_ref, *rest, with_perm):
    n_out = 1 + (N_GROUPS_B if with_perm else 0)
    outs = rest[:n_out]
    y_ref, os1_ref, os2_ref, tmp_ref, ls1_ref, ls2_ref, ltmp_ref, wb_ref = rest[n_out:]
    tm = x_ref.shape[0]

    @pl.when(pl.program_id(0) == 0)
    def _():
        wb_ref[...] = w_ref[...].astype(BF16)

    q4, q16 = tm // RES_STEP, tm // (RES_STEP * RES_STEP)
    for r4 in range(RES_STEP):
        ls1_ref[pl.ds(r4, q4, stride=RES_STEP), :] = l1_ref[r4]
        for a in range(RES_STEP):
            ltmp_ref[pl.ds(r4 * q4 + a, q16, stride=RES_STEP), :] = l2_ref[r4 + RES_STEP * a]
        ls2_ref[pl.ds(r4, q4, stride=RES_STEP), :] = ltmp_ref[r4 * q4:(r4 + 1) * q4, :]
    for s in range(N_SLABS):
        lanes = slice(s * LANES, (s + 1) * LANES)
        for r4 in range(RES_STEP):
            os1_ref[s, pl.ds(r4, q4, stride=RES_STEP), :] = o1_ref[r4, :, lanes].astype(F32)
            for a in range(RES_STEP):
                tmp_ref[s, pl.ds(r4 * q4 + a, q16, stride=RES_STEP), :] = (
                    o2_ref[r4 + RES_STEP * a, :, lanes].astype(F32))
            os2_ref[s, pl.ds(r4, q4, stride=RES_STEP), :] = tmp_ref[s, r4 * q4:(r4 + 1) * q4, :]
    stats = (l0_ref[...], ls1_ref[...], ls2_ref[...])
    maxes = [pltpu.roll(t, LANES - HEADS_B, 1) for t in stats]
    head_lane = lax.broadcasted_iota(jnp.int32, (1, LANES), 1) < HEADS_B
    lses = [m + jnp.log(jnp.where(head_lane, t, 1.0)) for m, t in zip(maxes, stats)]
    mx = jnp.maximum(jnp.maximum(lses[0], lses[1]), lses[2])
    inv = 1.0 / (jnp.exp(lses[0] - mx) + jnp.exp(lses[1] - mx) + jnp.exp(lses[2] - mx))
    w0, w1, w2 = [jnp.exp(m - mx) * inv for m in maxes]
    for h in range(HEADS_B):
        sl = slice(h * HEAD_DIM_B, (h + 1) * HEAD_DIM_B)
        o = (w0[:, h:h + 1] * o0_ref[:, sl].astype(F32) + w1[:, h:h + 1] * os1_ref[h]
             + w2[:, h:h + 1] * os2_ref[h])
        zz = z_ref[:, sl].astype(F32)
        y_ref[:, sl] = (o * (zz * _sigmoid(zz))).astype(BF16)
    r = ALPHA * x_ref[...] + _dot(y_ref[...], wb_ref[...])
    _store_stream(_layer_norm_rows(r, g_ref[...], b_ref[...]), outs[0], outs[1:], os1_ref, os2_ref)


def _merge_out_ln(os_, lses, z, w, layer, x, g, b, ln_layer, name, with_perm, tm=512):
    m = x.shape[0]
    n = D_MODEL
    per_seq = SEQ // tm
    tok = lambda width: pl.BlockSpec((tm, width), lambda i: (i, 0))
    res = lambda d, width: pl.BlockSpec((None, d, tm // d, width),
                                        lambda i: (i // per_seq, 0, i % per_seq, 0))
    vec = pl.BlockSpec((None, 1, n), lambda i: (ln_layer, 0, 0))
    d1, d2 = DILATIONS[1], DILATIONS[2]
    out_shape, out_specs = _stream_out(tm, with_perm)
    return pl.pallas_call(
        functools.partial(_merge_out_ln_kernel, with_perm=with_perm),
        grid=(m // tm,),
        in_specs=[tok(INNER_B), res(d1, INNER_B), res(d2, INNER_B),
                  tok(LANES), res(d1, LANES), res(d2, LANES),
                  tok(INNER_B),
                  pl.BlockSpec((None, INNER_B, n), lambda i: (layer, 0, 0)),
                  tok(n), vec, vec],
        out_specs=out_specs,
        out_shape=out_shape,
        scratch_shapes=[pltpu.VMEM((tm, INNER_B), BF16),
                        pltpu.VMEM((N_SLABS, tm, LANES), F32), pltpu.VMEM((N_SLABS, tm, LANES), F32),
                        pltpu.VMEM((N_SLABS, tm, LANES), F32),
                        pltpu.VMEM((tm, LANES), F32), pltpu.VMEM((tm, LANES), F32),
                        pltpu.VMEM((tm, LANES), F32), pltpu.VMEM((INNER_B, n), BF16)],
        compiler_params=pltpu.CompilerParams(dimension_semantics=("arbitrary",),
                                             vmem_limit_bytes=VMEM_LIMIT),
        name=name,
    )(os_[0], os_[1].reshape(BATCH, d1, SEQ // d1, INNER_B), os_[2].reshape(BATCH, d2, SEQ // d2, INNER_B),
      lses[0], lses[1].reshape(BATCH, d1, SEQ // d1, LANES), lses[2].reshape(BATCH, d2, SEQ // d2, LANES),
      z, w, x, g, b)


def _block_diag_lane_blocks(w):
    per = LANES // QKV_BLOCK
    w4 = w.reshape(-1, per, QKV_BLOCK, QKV_BLOCK)
    eye = jnp.eye(per, dtype=w.dtype)
    dense = jnp.einsum('bmij,mp->bmjpi', w4, eye)
    return dense.reshape(-1, LANES, LANES)


def _fold_gate_weights(wq, wk, wv, w_if):
    nblk = INNER_A // QKV_BLOCK
    wif = w_if.reshape(3, nblk, QKV_BLOCK, 2 * HEADS_A)
    hp = lax.Precision.HIGHEST
    wgc = (jnp.einsum('nij,nio->njo', wq, wif[0], precision=hp)
           + jnp.einsum('nij,nio->njo', wk, wif[1], precision=hp)).reshape(INNER_A, 2 * HEADS_A)
    wgm = jnp.einsum('nij,nio->njo', wv, wif[2], precision=hp).reshape(INNER_A, 2 * HEADS_A)
    pad = ((0, 0), (0, LANES - 2 * HEADS_A))
    return jnp.pad(wgc, pad).astype(BF16), jnp.pad(wgm, pad).astype(BF16)


def kernel(x, ln_g, ln_b, a_w_in, a_conv_w, a_conv_b, a_wq, a_wk, a_wv, a_w_if, a_b_if, a_gn_g, a_skip,
           a_w_out, b_w_kv, b_w_in, b_w_out):
    xs = x.reshape(TOKENS, D_MODEL)
    x_in = xs
    block_diag = jax.vmap(_block_diag_lane_blocks)
    bdqk = jnp.concatenate([block_diag(a_wq), block_diag(a_wk)], axis=-1).astype(BF16)
    bdv = block_diag(a_wv).astype(BF16)
    wgc, wgm = jax.vmap(_fold_gate_weights)(a_wq, a_wk, a_wv, a_w_if)
    bg = jnp.pad(a_b_if[:, None, :], ((0, 0), (0, 0), (0, LANES - 2 * HEADS_A)))
    conv_b, gn, skip = a_conv_b[:, None, :], a_gn_g[:, None, :], a_skip[:, None, :]
    ln_g3, ln_b3 = ln_g[:, None, :], ln_b[:, None, :]
    for layer in range(N_A_LAYERS):
        proj = _proj(x_in, a_w_in, layer, 0, 3 * INNER_A, f"a{layer}_proj")
        q, k, v, xc, wi, cv = _mlstm_pre(proj, layer, a_conv_w, conv_b, bdqk, bdv, wgc, wgm, bg,
                                         f"a{layer}_pre")
        y = _mlstm_core(q, k, v, xc, proj, wi, cv, layer, gn, skip, f"a{layer}_mlstm")
        outs = _out_ln(y, a_w_out, layer, xs, ln_g3, ln_b3, layer, f"a{layer}_out_ln",
                       with_perm=layer == N_A_LAYERS - 1)
        xs = outs[0]
        x_in = xs

    perms = [p.reshape(TOKENS, D_MODEL) for p in outs[1:]]
    kvs = [_proj(perms[g], b_w_kv[None], 0, 2 * g * INNER_B, 2 * INNER_B, f"kv_proj{g}")
           for g in range(N_GROUPS_B)]
    q_scale = HEAD_DIM_B ** -0.5
    for lb in range(N_B_LAYERS):
        layer = N_A_LAYERS + lb
        qs = [_proj(perms[g], b_w_in, lb, g * INNER_B, INNER_B, f"b{lb}_proj{g}", out_scale=q_scale)
              for g in range(N_GROUPS_B)]
        z = _proj(perms[0], b_w_in, lb, N_GROUPS_B * INNER_B, INNER_B, f"b{lb}_projz")
        os_, lses = [], []
        for g in range(N_GROUPS_B):
            o, lse = _attn_group(qs[g], kvs[g], g, f"b{lb}_attn{g}")
            os_.append(o)
            lses.append(lse)
        outs = _merge_out_ln(os_, lses, z, b_w_out, lb, xs, ln_g3, ln_b3, layer,
                             f"b{lb}_merge_out_ln", with_perm=lb < N_B_LAYERS - 1)
        xs = outs[0]
        perms = [p.reshape(TOKENS, D_MODEL) for p in outs[1:]]
    return xs.reshape(BATCH, SEQ, D_MODEL)
```

```python
import functools

import jax
import jax.numpy as jnp
from jax import lax
from jax.experimental import pallas as pl
from jax.experimental.pallas import tpu as pltpu

D_MODEL = 1024
BATCH = 4
SEQ = 4096
DEPTH = 4
N_A_LAYERS = DEPTH // 2
N_B_LAYERS = DEPTH - N_A_LAYERS
INNER_A = 2 * D_MODEL
HEADS_A = 4
HEAD_DIM_A = INNER_A // HEADS_A
QKV_BLOCK = 4
CONV_K = 4
HEAD_DIM_B = 128
HEADS_B = D_MODEL // HEAD_DIM_B
INNER_B = HEADS_B * HEAD_DIM_B
GROUPS_B = ((128, 1), (512, 4), (2048, 16))
N_GROUPS_B = len(GROUPS_B)
DILATIONS = tuple(d for _, d in GROUPS_B)
BLOCK_B = 128
RES_STEP = 4
assert DILATIONS == (1, RES_STEP, RES_STEP * RES_STEP)
ALPHA = (2 * DEPTH) ** 0.25
LN_EPS = 1e-5

TOKENS = BATCH * SEQ
LANES = 128
SUBLANES = 8
N_SLABS = D_MODEL // LANES
CHUNK_A = 256
N_LANE_BLOCKS_A = INNER_A // LANES
VMEM_LIMIT = 48 * 1024 * 1024
PROJ_TN = 1024
CV_W_INTER, CV_FLOOR, CV_W_STATE, CV_DECAY = 0, HEADS_A, 2 * HEADS_A, 3 * HEADS_A

F32 = jnp.float32
BF16 = jnp.bfloat16

assert all(w // d == BLOCK_B for w, d in GROUPS_B)


def _dot(a, b):
    return jnp.dot(a, b, preferred_element_type=F32)


def _dot_nt(a, b):
    return lax.dot_general(a, b, (((1,), (1,)), ((), ())), preferred_element_type=F32)


def _sigmoid(x):
    return 0.5 * jnp.tanh(0.5 * x) + 0.5


def _layer_norm_rows(r, g, b):
    mu = jnp.mean(r, axis=-1, keepdims=True)
    d = r - mu
    var = jnp.mean(d * d, axis=-1, keepdims=True)
    return d * lax.rsqrt(var + LN_EPS) * g + b


def _store_stream(r, x_ref, perm_refs, slab_ref, slab4_ref):
    x_ref[...] = r
    if not perm_refs:
        return
    xb1_ref, xb4_ref, xb16_ref = perm_refs
    xb1_ref[...] = r.astype(BF16)
    tm = r.shape[0]
    q4, q16 = tm // RES_STEP, tm // (RES_STEP * RES_STEP)
    for s in range(N_SLABS):
        lanes = slice(s * LANES, (s + 1) * LANES)
        slab_ref[s] = r[:, lanes]
        for r4 in range(RES_STEP):
            part = slab_ref[s, pl.ds(r4, q4, stride=RES_STEP), :]
            slab4_ref[s, r4 * q4:(r4 + 1) * q4, :] = part
            xb4_ref[r4, :, lanes] = part.astype(BF16)
        for r4 in range(RES_STEP):
            for a in range(RES_STEP):
                xb16_ref[r4 + RES_STEP * a, :, lanes] = (
                    slab4_ref[s, pl.ds(r4 * q4 + a, q16, stride=RES_STEP), :].astype(BF16))


def _stream_out(tm, with_perm):
    per_seq = SEQ // tm
    shapes = [jax.ShapeDtypeStruct((TOKENS, D_MODEL), F32)]
    specs = [pl.BlockSpec((tm, D_MODEL), lambda i: (i, 0))]
    if with_perm:
        shapes.append(jax.ShapeDtypeStruct((TOKENS, D_MODEL), BF16))
        specs.append(pl.BlockSpec((tm, D_MODEL), lambda i: (i, 0)))
        for d in DILATIONS[1:]:
            shapes.append(jax.ShapeDtypeStruct((BATCH, d, SEQ // d, D_MODEL), BF16))
            specs.append(pl.BlockSpec((None, d, tm // d, D_MODEL),
                                      lambda i: (i // per_seq, 0, i % per_seq, 0)))
    return shapes, specs


def _proj_kernel(x_ref, w_ref, o_ref, wb_ref, *, out_scale):
    @pl.when(pl.program_id(1) == 0)
    def _():
        wb_ref[...] = w_ref[...].astype(BF16)

    acc = _dot(x_ref[...].astype(BF16), wb_ref[...])
    if out_scale is not None:
        acc = acc * out_scale
    o_ref[...] = acc.astype(o_ref.dtype)


def _proj(x, w, layer, col0, n, name, out_scale=None, tm=2048, tn=PROJ_TN):
    m, k = x.shape
    col_blk0 = col0 // tn
    return pl.pallas_call(
        functools.partial(_proj_kernel, out_scale=out_scale),
        grid=(n // tn, m // tm),
        in_specs=[pl.BlockSpec((tm, k), lambda j, i: (i, 0)),
                  pl.BlockSpec((None, k, tn), lambda j, i: (layer, 0, col_blk0 + j))],
        out_specs=pl.BlockSpec((tm, tn), lambda j, i: (i, j)),
        out_shape=jax.ShapeDtypeStruct((m, n), BF16),
        scratch_shapes=[pltpu.VMEM((k, tn), BF16)],
        compiler_params=pltpu.CompilerParams(dimension_semantics=("parallel", "arbitrary"),
                                             vmem_limit_bytes=VMEM_LIMIT),
        name=name,
    )(x, w)


def _pre_kernel(xm_ref, halo_ref, cw_ref, cb_ref, bdqk_ref, bdv_ref, wgc_ref, wgm_ref, bg_ref,
                q_ref, k_ref, v_ref, xc_ref, wi_ref, cv_ref, ext_ref, m_ref, gates_ref, *, n_tiles):
    tm = xm_ref.shape[0]
    i = pl.program_id(0)
    tile = jnp.minimum(i, n_tiles - 1)
    seq_start = (tile * tm) % SEQ == 0
    prev_seq_start = ((i - 1) * tm) % SEQ == 0

    @pl.when(i == 0)
    def _():
        gates_ref[...] = jnp.zeros_like(gates_ref)
        m_ref[...] = jnp.zeros_like(m_ref)

    m_start = jnp.where(prev_seq_start, 0.0, m_ref[0:1, :])
    m_ref[0:1, :] = _gate_weights(gates_ref[...], m_start, wi_ref, cv_ref)

    halo = jnp.where(seq_start, 0.0, halo_ref[...].astype(F32))
    for blk in range(N_LANE_BLOCKS_A):
        sl = slice(blk * LANES, (blk + 1) * LANES)
        ext_ref[blk, pl.ds(0, SUBLANES, stride=2), :] = halo[:, sl]
        ext_ref[blk, pl.ds(2 * SUBLANES, tm, stride=2), :] = xm_ref[:, sl].astype(F32)
        acc = cb_ref[:, sl]
        for j in range(CONV_K):
            off = 2 * (SUBLANES - (CONV_K - 1) + j)
            acc = acc + cw_ref[j:j + 1, sl] * ext_ref[blk, pl.ds(off, tm, stride=2), :]
        xc_ref[:, sl] = (acc * _sigmoid(acc)).astype(BF16)
        qk = _dot(xc_ref[:, sl], bdqk_ref[blk])
        q_ref[:, sl] = qk[:, :LANES].astype(BF16)
        k_ref[:, sl] = qk[:, LANES:].astype(BF16)
        v_ref[:, sl] = _dot(xm_ref[:, sl], bdv_ref[blk]).astype(BF16)

    gates_ref[...] = _dot(xc_ref[...], wgc_ref[...]) + _dot(xm_ref[...], wgm_ref[...]) + bg_ref[...]


def _gate_weights(gates, m_start, wi_ref, cv_ref):
    tm = gates.shape[0]
    L = CHUNK_A
    scale = HEAD_DIM_A ** -0.5
    log_sig = jnp.minimum(gates, 0.0) - jnp.log(1.0 + jnp.exp(-jnp.abs(gates)))
    log_f = pltpu.roll(log_sig, LANES - HEADS_A, 1)
    row = lax.broadcasted_iota(jnp.int32, (L, L), 0)
    col = lax.broadcasted_iota(jnp.int32, (L, L), 1)
    tri = col <= row
    tri_b = jnp.where(tri, 1.0, 0.0).astype(BF16)
    lane = lax.broadcasted_iota(jnp.int32, (L, LANES), 1)
    head_lane = lax.broadcasted_iota(jnp.int32, (1, LANES), 1) < HEADS_A
    chunks = [slice(c * L, (c + 1) * L) for c in range(tm // L)]

    parts = []
    for rows in chunks:
        li = gates[rows, :]
        lf = log_f[rows, :]
        lf_hi = lf.astype(BF16)
        lf_mid = (lf - lf_hi.astype(F32)).astype(BF16)
        lf_lo = (lf - lf_hi.astype(F32) - lf_mid.astype(F32)).astype(BF16)
        b = _dot(tri_b, lf_hi) + _dot(tri_b, lf_mid) + _dot(tri_b, lf_lo)
        g_tot = b[L - 1:L, :]
        li_rows = li.T[0:SUBLANES, :]
        b_rows = b.T[0:SUBLANES, :]
        row_max = jnp.zeros((L, LANES), F32)
        for h in range(HEADS_A):
            d_intra = jnp.where(tri, b[:, h:h + 1] - b_rows[h:h + 1, :] + li_rows[h:h + 1, :], -jnp.inf)
            wi_ref[h, rows, :] = d_intra
            row_max = jnp.where(lane == h, jnp.max(d_intra, axis=1, keepdims=True), row_max)
        lw = g_tot - b + li
        parts.append((b, g_tot, lw, jnp.max(lw, axis=0, keepdims=True), row_max))

    m_list = [m_start]
    for b, g_tot, lw, lw_max, row_max in parts:
        m_list.append(jnp.where(head_lane, jnp.maximum(g_tot + m_list[-1], lw_max), 0.0))

    for c, rows in enumerate(chunks):
        b, g_tot, lw, lw_max, row_max = parts[c]
        m_prev, m_new = m_list[c], m_list[c + 1]
        a_inter = b + m_prev
        m_t = jnp.maximum(a_inter, row_max)
        for h in range(HEADS_A):
            wi_ref[h, rows, :] = jnp.exp(wi_ref[h, rows, :] - m_t[:, h:h + 1]) * scale
        w_inter = jnp.exp(a_inter - m_t)
        floor = jnp.exp(-m_t)
        w_state = jnp.exp(lw - m_new) * scale
        decay = jnp.broadcast_to(jnp.exp(g_tot + m_prev - m_new), (L, LANES))
        cv_ref[rows, :] = jnp.where(
            lane < CV_FLOOR, w_inter,
            jnp.where(lane < CV_W_STATE, pltpu.roll(floor, CV_FLOOR, 1),
                      jnp.where(lane < CV_DECAY, pltpu.roll(w_state, CV_W_STATE, 1),
                                jnp.where(lane < CV_DECAY + HEADS_A, pltpu.roll(decay, CV_DECAY, 1),
                                          0.0))))
    return m_list[-1]


def _mlstm_pre(proj, layer, cw, cb, bdqk, bdv, wgc, wgm, bg, name, tm=512):
    n_rows = proj.shape[0]
    n_tiles = n_rows // tm
    act = jax.ShapeDtypeStruct((n_rows, INNER_A), BF16)
    cur = lambda i: jnp.minimum(i, n_tiles - 1)
    prev = lambda i: jnp.maximum(i - 1, 0)
    row_spec = pl.BlockSpec((tm, INNER_A), lambda i: (cur(i), 0))
    full = lambda shape: pl.BlockSpec((None,) + shape, lambda i: (layer,) + (0,) * len(shape))
    return pl.pallas_call(
        functools.partial(_pre_kernel, n_tiles=n_tiles),
        grid=(n_tiles + 1,),
        in_specs=[row_spec,
                  pl.BlockSpec((SUBLANES, INNER_A),
                               lambda i: (jnp.maximum(cur(i) * (tm // SUBLANES) - 1, 0), 0)),
                  full((CONV_K, INNER_A)), full((1, INNER_A)),
                  full((N_LANE_BLOCKS_A, LANES, 2 * LANES)), full((N_LANE_BLOCKS_A, LANES, LANES)),
                  full((INNER_A, LANES)), full((INNER_A, LANES)), full((1, LANES))],
        out_specs=[row_spec, row_spec, row_spec, row_spec,
                   pl.BlockSpec((HEADS_A, tm, CHUNK_A), lambda i: (0, prev(i), 0)),
                   pl.BlockSpec((tm, LANES), lambda i: (prev(i), 0))],
        out_shape=[act, act, act, act,
                   jax.ShapeDtypeStruct((HEADS_A, n_rows, CHUNK_A), F32),
                   jax.ShapeDtypeStruct((n_rows, LANES), F32)],
        scratch_shapes=[pltpu.VMEM((N_LANE_BLOCKS_A, 2 * (tm + SUBLANES), LANES), F32),
                        pltpu.VMEM((SUBLANES, LANES), F32), pltpu.VMEM((tm, LANES), F32)],
        compiler_params=pltpu.CompilerParams(dimension_semantics=("arbitrary",),
                                             vmem_limit_bytes=VMEM_LIMIT),
        name=name,
    )(proj, proj, cw, cb, bdqk, bdv, wgc, wgm, bg)


def _mlstm_kernel(q_ref, k_ref, v_ref, xc_ref, z_ref, o_ref, wi_ref, cv_ref, gn_ref, sk_ref, y_ref,
                  ct_ref, n_ref):
    @pl.when(pl.program_id(1) == 0)
    def _():
        ct_ref[...] = jnp.zeros_like(ct_ref)
        n_ref[...] = jnp.zeros_like(n_ref)

    for c in range(q_ref.shape[0] // CHUNK_A):
        r = slice(c * CHUNK_A, (c + 1) * CHUNK_A)
        for h in range(HEADS_A):
            hs = slice(h * HEAD_DIM_A, (h + 1) * HEAD_DIM_A)
            qb = q_ref[r, hs]
            kb = k_ref[r, hs]
            vb = v_ref[r, hs]
            w_inter = cv_ref[r, CV_W_INTER + h:CV_W_INTER + h + 1]
            floor = cv_ref[r, CV_FLOOR + h:CV_FLOOR + h + 1]
            w_state = cv_ref[r, CV_W_STATE + h:CV_W_STATE + h + 1]
            decay = cv_ref[c * CHUNK_A:c * CHUNK_A + 1, CV_DECAY + h:CV_DECAY + h + 1]

            sc = _dot_nt(qb, kb) * wi_ref[h, r, :]
            num = w_inter * _dot(qb, ct_ref[h].astype(BF16)) + _dot(sc.astype(BF16), vb)
            n_rows = jnp.broadcast_to(n_ref[h].astype(BF16), (LANES, HEAD_DIM_A))
            den = w_inter * _dot_nt(qb, n_rows)[:, 0:1] + jnp.sum(sc, axis=1, keepdims=True)
            h_tilde = num * (1.0 / jnp.maximum(jnp.abs(den), floor))

            kw = kb.astype(F32) * w_state
            ct_ref[h] = decay * ct_ref[h] + lax.dot_general(
                kw.astype(BF16), vb, (((0,), (0,)), ((), ())), preferred_element_type=F32)
            n_ref[h] = decay * n_ref[h] + jnp.sum(kw, axis=0, keepdims=True)

            hg = h_tilde * _sigmoid(o_ref[r, hs].astype(F32))
            mu = jnp.mean(hg, axis=1, keepdims=True)
            dlt = hg - mu
            var = jnp.mean(dlt * dlt, axis=1, keepdims=True)
            hn = dlt * lax.rsqrt(var + LN_EPS) * gn_ref[:, hs]
            zz = z_ref[r, hs].astype(F32)
            y = (hn + sk_ref[:, hs] * xc_ref[r, hs].astype(F32)) * (zz * _sigmoid(zz))
            y_ref[r, hs] = y.astype(BF16)


def _mlstm_core(q, k, v, xc, proj, wi, cv, layer, gn, skip, name, chunks_per_step=2):
    L = chunks_per_step * CHUNK_A
    dh = HEAD_DIM_A
    steps = SEQ // L
    row = lambda b, c: b * steps + c
    act_spec = pl.BlockSpec((L, INNER_A), lambda b, c: (row(b, c), 0))
    vec_spec = pl.BlockSpec((None, 1, INNER_A), lambda b, c: (layer, 0, 0))
    return pl.pallas_call(
        _mlstm_kernel,
        grid=(BATCH, steps),
        in_specs=[act_spec, act_spec, act_spec, act_spec,
                  pl.BlockSpec((L, INNER_A), lambda b, c: (row(b, c), 1)),
                  pl.BlockSpec((L, INNER_A), lambda b, c: (row(b, c), 2)),
                  pl.BlockSpec((HEADS_A, L, CHUNK_A), lambda b, c: (0, row(b, c), 0)),
                  pl.BlockSpec((L, LANES), lambda b, c: (row(b, c), 0)),
                  vec_spec, vec_spec],
        out_specs=act_spec,
        out_shape=jax.ShapeDtypeStruct((TOKENS, INNER_A), BF16),
        scratch_shapes=[pltpu.VMEM((HEADS_A, dh, dh), F32), pltpu.VMEM((HEADS_A, 1, dh), F32)],
        compiler_params=pltpu.CompilerParams(
            dimension_semantics=("parallel", "arbitrary"), vmem_limit_bytes=VMEM_LIMIT),
        name=name,
    )(q, k, v, xc, proj, proj, wi, cv, gn, skip)


def _out_ln_kernel(y_ref, w_ref, x_ref, g_ref, b_ref, *rest, with_perm):
    n_out = 1 + (N_GROUPS_B if with_perm else 0)
    outs, scratch = rest[:n_out], rest[n_out:]
    wb_ref = scratch[0]

    @pl.when(pl.program_id(0) == 0)
    def _():
        wb_ref[...] = w_ref[...].astype(BF16)

    r = ALPHA * x_ref[...] + _dot(y_ref[...], wb_ref[...])
    slabs = scratch[1:] if with_perm else (None, None)
    _store_stream(_layer_norm_rows(r, g_ref[...], b_ref[...]), outs[0], outs[1:], *slabs)


def _out_ln(y, w, layer, x, g, b, ln_layer, name, with_perm, tm=512):
    m, k = y.shape
    n = w.shape[2]
    out_shape, out_specs = _stream_out(tm, with_perm)
    return pl.pallas_call(
        functools.partial(_out_ln_kernel, with_perm=with_perm),
        grid=(m // tm,),
        in_specs=[pl.BlockSpec((tm, k), lambda i: (i, 0)),
                  pl.BlockSpec((None, k, n), lambda i: (layer, 0, 0)),
                  pl.BlockSpec((tm, n), lambda i: (i, 0)),
                  pl.BlockSpec((None, 1, n), lambda i: (ln_layer, 0, 0)),
                  pl.BlockSpec((None, 1, n), lambda i: (ln_layer, 0, 0))],
        out_specs=out_specs,
        out_shape=out_shape,
        scratch_shapes=([pltpu.VMEM((k, n), BF16)]
                        + ([pltpu.VMEM((N_SLABS, tm, LANES), F32)] * 2 if with_perm else [])),
        compiler_params=pltpu.CompilerParams(dimension_semantics=("arbitrary",),
                                             vmem_limit_bytes=VMEM_LIMIT),
        name=name,
    )(y, w, x, g, b)


def _attn_kernel(q_ref, kvp_ref, kvc_ref, o_ref, lse_ref, *, dilation, blocks_per_seq):
    n = BLOCK_B
    n_blocks = q_ref.shape[0] // n
    qi = lax.broadcasted_iota(jnp.int32, (n, n), 0)
    kj = lax.broadcasted_iota(jnp.int32, (n, n), 1)
    lower = kj <= qi
    diag = kj == qi
    dist = jnp.bitwise_and(qi - kj, n - 1).astype(F32)
    heads = range(HEADS_B)
    ksl = [slice(h * HEAD_DIM_B, (h + 1) * HEAD_DIM_B) for h in heads]
    vsl = [slice(INNER_B + h * HEAD_DIM_B, INNER_B + (h + 1) * HEAD_DIM_B) for h in heads]
    slopes = [2.0 ** (-8.0 * (h + 1.0) / HEADS_B) * dilation for h in heads]

    for blk in range(n_blocks):
        rows = slice(blk * n, (blk + 1) * n)
        prev_ref, prev_rows = ((kvp_ref, slice(0, n)) if blk == 0
                               else (kvc_ref, slice((blk - 1) * n, blk * n)))
        first = (pl.program_id(0) * n_blocks + blk) % blocks_per_seq == 0
        prev_bias = jnp.where(first, -jnp.inf, 0.0)

        both_rows = slice((blk - 1) * n, (blk + 1) * n)
        scores = []
        for h in heads:
            qh = q_ref[rows, ksl[h]]
            if blk == 0:
                s_prev = _dot_nt(qh, prev_ref[prev_rows, ksl[h]])
                s_cur = _dot_nt(qh, kvc_ref[rows, ksl[h]])
            else:
                s_both = _dot_nt(qh, kvc_ref[both_rows, ksl[h]])
                s_prev, s_cur = s_both[:, :n], s_both[:, n:]
            s_prev = s_prev + prev_bias
            s = jnp.where(lower, s_cur, s_prev) - slopes[h] * dist
            s_diag = (jnp.max(jnp.where(diag, s_prev, -jnp.inf), axis=-1, keepdims=True)
                      - slopes[h] * float(n))
            m = jnp.maximum(jnp.max(s, axis=-1, keepdims=True), s_diag)
            scores.append((s, s_diag, m))
        probs = []
        for h in heads:
            s, s_diag, m = scores[h]
            e = jnp.exp(s - m)
            e_diag = jnp.exp(s_diag - m)
            den = jnp.sum(e, axis=-1, keepdims=True) + e_diag
            p_prev = jnp.where(lower, jnp.where(diag, e_diag, 0.0), e)
            probs.append((jnp.where(lower, e, 0.0).astype(BF16), p_prev.astype(BF16), den))
        stats = jnp.ones((n, LANES), F32)
        for h in heads:
            p_cur, p_prev, den = probs[h]
            if blk == 0:
                o = _dot(p_cur, kvc_ref[rows, vsl[h]]) + _dot(p_prev, prev_ref[prev_rows, vsl[h]])
            else:
                o = _dot(jnp.concatenate([p_prev, p_cur], axis=1), kvc_ref[both_rows, vsl[h]])
            o_ref[rows, ksl[h]] = o.astype(BF16)
            stats = jnp.where(kj == h, den, jnp.where(kj == HEADS_B + h, scores[h][2], stats))
        lse_ref[rows, :] = stats


def _attn_group(q, kv, group, name, blocks_per_step=8):
    dilation = DILATIONS[group]
    blocks_per_seq = SEQ // dilation // BLOCK_B
    tq = blocks_per_step * BLOCK_B
    return pl.pallas_call(
        functools.partial(_attn_kernel, dilation=float(dilation), blocks_per_seq=blocks_per_seq),
        grid=(TOKENS // tq,),
        in_specs=[pl.BlockSpec((tq, INNER_B), lambda n: (n, 0)),
                  pl.BlockSpec((BLOCK_B, 2 * INNER_B),
                               lambda n: (jnp.maximum(n * blocks_per_step - 1, 0), 0)),
                  pl.BlockSpec((tq, 2 * INNER_B), lambda n: (n, 0))],
        out_specs=[pl.BlockSpec((tq, INNER_B), lambda n: (n, 0)),
                   pl.BlockSpec((tq, LANES), lambda n: (n, 0))],
        out_shape=[jax.ShapeDtypeStruct((TOKENS, INNER_B), BF16),
                   jax.ShapeDtypeStruct((TOKENS, LANES), F32)],
        compiler_params=pltpu.CompilerParams(dimension_semantics=("arbitrary",),
                                             vmem_limit_bytes=VMEM_LIMIT),
        name=name,
    )(q, kv, kv)


def _merge_out_ln_kernel(o0_ref, o1_ref, o2_ref, l0_ref, l1_ref, l2_ref, z_ref, w_ref, x_ref,
                         g_ref, b_ref, *rest, with_perm):
    n_out = 1 + (N_GROUPS_B if with_perm else 0)
    outs = rest[:n_out]
    y_ref, os1_ref, os2_ref, tmp_ref, ls1_ref, ls2_ref, ltmp_ref, wb_ref = rest[n_out:]
    tm = x_ref.shape[0]

    @pl.when(pl.program_id(0) == 0)
    def _():
        wb_ref[...] = w_ref[...].astype(BF16)

    q4, q16 = tm // RES_STEP, tm // (RES_STEP * RES_STEP)
    for r4 in range(RES_STEP):
        ls1_ref[pl.ds(r4, q4, stride=RES_STEP), :] = l1_ref[r4]
        for a in range(RES_STEP):
            ltmp_ref[pl.ds(r4 * q4 + a, q16, stride=RES_STEP), :] = l2_ref[r4 + RES_STEP * a]
        ls2_ref[pl.ds(r4, q4, stride=RES_STEP), :] = ltmp_ref[r4 * q4:(r4 + 1) * q4, :]
    for s in range(N_SLABS):
        lanes = slice(s * LANES, (s + 1) * LANES)
        for r4 in range(RES_STEP):
            os1_ref[s, pl.ds(r4, q4, stride=RES_STEP), :] = o1_ref[r4, :, lanes].astype(F32)
            for a in range(RES_STEP):
                tmp_ref[s, pl.ds(r4 * q4 + a, q16, stride=RES_STEP), :] = (
                    o2_ref[r4 + RES_STEP * a, :, lanes].astype(F32))
            os2_ref[s, pl.ds(r4, q4, stride=RES_STEP), :] = tmp_ref[s, r4 * q4:(r4 + 1) * q4, :]
    stats = (l0_ref[...], ls1_ref[...], ls2_ref[...])
    maxes = [pltpu.roll(t, LANES - HEADS_B, 1) for t in stats]
    head_lane = lax.broadcasted_iota(jnp.int32, (1, LANES), 1) < HEADS_B
    lses = [m + jnp.log(jnp.where(head_lane, t, 1.0)) for m, t in zip(maxes, stats)]
    mx = jnp.maximum(jnp.maximum(lses[0], lses[1]), lses[2])
    inv = 1.0 / (jnp.exp(lses[0] - mx) + jnp.exp(lses[1] - mx) + jnp.exp(lses[2] - mx))
    w0, w1, w2 = [jnp.exp(m - mx) * inv for m in maxes]
    for h in range(HEADS_B):
        sl = slice(h * HEAD_DIM_B, (h + 1) * HEAD_DIM_B)
        o = (w0[:, h:h + 1] * o0_ref[:, sl].astype(F32) + w1[:, h:h + 1] * os1_ref[h]
             + w2[:, h:h + 1] * os2_ref[h])
        zz = z_ref[:, sl].astype(F32)
        y_ref[:, sl] = (o * (zz * _sigmoid(zz))).astype(BF16)
    r = ALPHA * x_ref[...] + _dot(y_ref[...], wb_ref[...])
    _store_stream(_layer_norm_rows(r, g_ref[...], b_ref[...]), outs[0], outs[1:], os1_ref, os2_ref)


def _merge_out_ln(os_, lses, z, w, layer, x, g, b, ln_layer, name, with_perm, tm=512):
    m = x.shape[0]
    n = D_MODEL
    per_seq = SEQ // tm
    tok = lambda width: pl.BlockSpec((tm, width), lambda i: (i, 0))
    res = lambda d, width: pl.BlockSpec((None, d, tm // d, width),
                                        lambda i: (i // per_seq, 0, i % per_seq, 0))
    vec = pl.BlockSpec((None, 1, n), lambda i: (ln_layer, 0, 0))
    d1, d2 = DILATIONS[1], DILATIONS[2]
    out_shape, out_specs = _stream_out(tm, with_perm)
    return pl.pallas_call(
        functools.partial(_merge_out_ln_kernel, with_perm=with_perm),
        grid=(m // tm,),
        in_specs=[tok(INNER_B), res(d1, INNER_B), res(d2, INNER_B),
                  tok(LANES), res(d1, LANES), res(d2, LANES),
                  tok(INNER_B),
                  pl.BlockSpec((None, INNER_B, n), lambda i: (layer, 0, 0)),
                  tok(n), vec, vec],
        out_specs=out_specs,
        out_shape=out_shape,
        scratch_shapes=[pltpu.VMEM((tm, INNER_B), BF16),
                        pltpu.VMEM((N_SLABS, tm, LANES), F32), pltpu.VMEM((N_SLABS, tm, LANES), F32),
                        pltpu.VMEM((N_SLABS, tm, LANES), F32),
                        pltpu.VMEM((tm, LANES), F32), pltpu.VMEM((tm, LANES), F32),
                        pltpu.VMEM((tm, LANES), F32), pltpu.VMEM((INNER_B, n), BF16)],
        compiler_params=pltpu.CompilerParams(dimension_semantics=("arbitrary",),
                                             vmem_limit_bytes=VMEM_LIMIT),
        name=name,
    )(os_[0], os_[1].reshape(BATCH, d1, SEQ // d1, INNER_B), os_[2].reshape(BATCH, d2, SEQ // d2, INNER_B),
      lses[0], lses[1].reshape(BATCH, d1, SEQ // d1, LANES), lses[2].reshape(BATCH, d2, SEQ // d2, LANES),
      z, w, x, g, b)


def _block_diag_lane_blocks(w):
    per = LANES // QKV_BLOCK
    w4 = w.reshape(-1, per, QKV_BLOCK, QKV_BLOCK)
    eye = jnp.eye(per, dtype=w.dtype)
    dense = jnp.einsum('bmij,mp->bmjpi', w4, eye)
    return dense.reshape(-1, LANES, LANES)


def _fold_gate_weights(wq, wk, wv, w_if):
    nblk = INNER_A // QKV_BLOCK
    wif = w_if.reshape(3, nblk, QKV_BLOCK, 2 * HEADS_A)
    hp = lax.Precision.HIGHEST
    wgc = (jnp.einsum('nij,nio->njo', wq, wif[0], precision=hp)
           + jnp.einsum('nij,nio->njo', wk, wif[1], precision=hp)).reshape(INNER_A, 2 * HEADS_A)
    wgm = jnp.einsum('nij,nio->njo', wv, wif[2], precision=hp).reshape(INNER_A, 2 * HEADS_A)
    pad = ((0, 0), (0, LANES - 2 * HEADS_A))
    return jnp.pad(wgc, pad).astype(BF16), jnp.pad(wgm, pad).astype(BF16)


def kernel(x, ln_g, ln_b, a_w_in, a_conv_w, a_conv_b, a_wq, a_wk, a_wv, a_w_if, a_b_if, a_gn_g, a_skip,
           a_w_out, b_w_kv, b_w_in, b_w_out):
    xs = x.reshape(TOKENS, D_MODEL)
    x_in = xs
    block_diag = jax.vmap(_block_diag_lane_blocks)
    bdqk = jnp.concatenate([block_diag(a_wq), block_diag(a_wk)], axis=-1).astype(BF16)
    bdv = block_diag(a_wv).astype(BF16)
    wgc, wgm = jax.vmap(_fold_gate_weights)(a_wq, a_wk, a_wv, a_w_if)
    bg = jnp.pad(a_b_if[:, None, :], ((0, 0), (0, 0), (0, LANES - 2 * HEADS_A)))
    conv_b, gn, skip = a_conv_b[:, None, :], a_gn_g[:, None, :], a_skip[:, None, :]
    ln_g3, ln_b3 = ln_g[:, None, :], ln_b[:, None, :]
    for layer in range(N_A_LAYERS):
        proj = _proj(x_in, a_w_in, layer, 0, 3 * INNER_A, f"a{layer}_proj")
        q, k, v, xc, wi, cv = _mlstm_pre(proj, layer, a_conv_w, conv_b, bdqk, bdv, wgc, wgm, bg,
                                         f"a{layer}_pre")
        y = _mlstm_core(q, k, v, xc, proj, wi, cv, layer, gn, skip, f"a{layer}_mlstm")
        outs = _out_ln(y, a_w_out, layer, xs, ln_g3, ln_b3, layer, f"a{layer}_out_ln",
                       with_perm=layer == N_A_LAYERS - 1)
        xs = outs[0]
        x_in = xs

    perms = [p.reshape(TOKENS, D_MODEL) for p in outs[1:]]
    kvs = [_proj(perms[g], b_w_kv[None], 0, 2 * g * INNER_B, 2 * INNER_B, f"kv_proj{g}")
           for g in range(N_GROUPS_B)]
    q_scale = HEAD_DIM_B ** -0.5
    for lb in range(N_B_LAYERS):
        layer = N_A_LAYERS + lb
        qs = [_proj(perms[g], b_w_in, lb, g * INNER_B, INNER_B, f"b{lb}_proj{g}", out_scale=q_scale)
              for g in range(N_GROUPS_B)]
        z = _proj(perms[0], b_w_in, lb, N_GROUPS_B * INNER_B, INNER_B, f"b{lb}_projz")
        os_, lses = [], []
        for g in range(N_GROUPS_B):
            o, lse = _attn_group(qs[g], kvs[g], g, f"b{lb}_attn{g}")
            os_.append(o)
            lses.append(lse)
        outs = _merge_out_ln(os_, lses, z, b_w_out, lb, xs, ln_g3, ln_b3, layer,
                             f"b{lb}_merge_out_ln", with_perm=lb < N_B_LAYERS - 1)
        xs = outs[0]
        perms = [p.reshape(TOKENS, D_MODEL) for p in outs[1:]]
    return xs.reshape(BATCH, SEQ, D_MODEL)
```

```python
import functools

import jax
import jax.numpy as jnp
from jax import lax
from jax.experimental import pallas as pl
from jax.experimental.pallas import tpu as pltpu

D_MODEL = 1024
BATCH = 4
SEQ = 4096
DEPTH = 4
N_A_LAYERS = DEPTH // 2
N_B_LAYERS = DEPTH - N_A_LAYERS
INNER_A = 2 * D_MODEL
HEADS_A = 4
HEAD_DIM_A = INNER_A // HEADS_A
QKV_BLOCK = 4
CONV_K = 4
HEAD_DIM_B = 128
HEADS_B = D_MODEL // HEAD_DIM_B
INNER_B = HEADS_B * HEAD_DIM_B
GROUPS_B = ((128, 1), (512, 4), (2048, 16))
N_GROUPS_B = len(GROUPS_B)
DILATIONS = tuple(d for _, d in GROUPS_B)
BLOCK_B = 128
RES_STEP = 4
assert DILATIONS == (1, RES_STEP, RES_STEP * RES_STEP)
ALPHA = (2 * DEPTH) ** 0.25
LN_EPS = 1e-5

TOKENS = BATCH * SEQ
LANES = 128
SUBLANES = 8
N_SLABS = D_MODEL // LANES
CHUNK_A = 256
N_LANE_BLOCKS_A = INNER_A // LANES
VMEM_LIMIT = 48 * 1024 * 1024
PROJ_TN = 1024
CV_W_INTER, CV_FLOOR, CV_W_STATE, CV_DECAY = 0, HEADS_A, 2 * HEADS_A, 3 * HEADS_A

F32 = jnp.float32
BF16 = jnp.bfloat16

assert all(w // d == BLOCK_B for w, d in GROUPS_B)


def _dot(a, b):
    return jnp.dot(a, b, preferred_element_type=F32)


def _dot_nt(a, b):
    return lax.dot_general(a, b, (((1,), (1,)), ((), ())), preferred_element_type=F32)


def _sigmoid(x):
    return 0.5 * jnp.tanh(0.5 * x) + 0.5


def _layer_norm_rows(r, g, b):
    mu = jnp.mean(r, axis=-1, keepdims=True)
    d = r - mu
    var = jnp.mean(d * d, axis=-1, keepdims=True)
    return d * lax.rsqrt(var + LN_EPS) * g + b


def _store_stream(r, x_ref, perm_refs, slab_ref, slab4_ref):
    x_ref[...] = r
    if not perm_refs:
        return
    xb1_ref, xb4_ref, xb16_ref = perm_refs
    xb1_ref[...] = r.astype(BF16)
    tm = r.shape[0]
    q4, q16 = tm // RES_STEP, tm // (RES_STEP * RES_STEP)
    for s in range(N_SLABS):
        lanes = slice(s * LANES, (s + 1) * LANES)
        slab_ref[s] = r[:, lanes]
        for r4 in range(RES_STEP):
            part = slab_ref[s, pl.ds(r4, q4, stride=RES_STEP), :]
            slab4_ref[s, r4 * q4:(r4 + 1) * q4, :] = part
            xb4_ref[r4, :, lanes] = part.astype(BF16)
        for r4 in range(RES_STEP):
            for a in range(RES_STEP):
                xb16_ref[r4 + RES_STEP * a, :, lanes] = (
                    slab4_ref[s, pl.ds(r4 * q4 + a, q16, stride=RES_STEP), :].astype(BF16))


def _stream_out(tm, with_perm):
    per_seq = SEQ // tm
    shapes = [jax.ShapeDtypeStruct((TOKENS, D_MODEL), F32)]
    specs = [pl.BlockSpec((tm, D_MODEL), lambda i: (i, 0))]
    if with_perm:
        shapes.append(jax.ShapeDtypeStruct((TOKENS, D_MODEL), BF16))
        specs.append(pl.BlockSpec((tm, D_MODEL), lambda i: (i, 0)))
        for d in DILATIONS[1:]:
            shapes.append(jax.ShapeDtypeStruct((BATCH, d, SEQ // d, D_MODEL), BF16))
            specs.append(pl.BlockSpec((None, d, tm // d, D_MODEL),
                                      lambda i: (i // per_seq, 0, i % per_seq, 0)))
    return shapes, specs


def _proj_kernel(x_ref, w_ref, o_ref, wb_ref, *, out_scale):
    @pl.when(pl.program_id(1) == 0)
    def _():
        wb_ref[...] = w_ref[...].astype(BF16)

    acc = _dot(x_ref[...].astype(BF16), wb_ref[...])
    if out_scale is not None:
        acc = acc * out_scale
    o_ref[...] = acc.astype(o_ref.dtype)


def _proj(x, w, layer, col0, n, name, out_scale=None, tm=2048, tn=PROJ_TN):
    m, k = x.shape
    col_blk0 = col0 // tn
    return pl.pallas_call(
        functools.partial(_proj_kernel, out_scale=out_scale),
        grid=(n // tn, m // tm),
        in_specs=[pl.BlockSpec((tm, k), lambda j, i: (i, 0)),
                  pl.BlockSpec((None, k, tn), lambda j, i: (layer, 0, col_blk0 + j))],
        out_specs=pl.BlockSpec((tm, tn), lambda j, i: (i, j)),
        out_shape=jax.ShapeDtypeStruct((m, n), BF16),
        scratch_shapes=[pltpu.VMEM((k, tn), BF16)],
        compiler_params=pltpu.CompilerParams(dimension_semantics=("parallel", "arbitrary"),
                                             vmem_limit_bytes=VMEM_LIMIT),
        name=name,
    )(x, w)


def _pre_kernel(xm_ref, halo_ref, cw_ref, cb_ref, bdqk_ref, bdv_ref, wgc_ref, wgm_ref, bg_ref,
                q_ref, k_ref, v_ref, xc_ref, wi_ref, cv_ref, ext_ref, m_ref, gates_ref, *, n_tiles):
    tm = xm_ref.shape[0]
    i = pl.program_id(0)
    tile = jnp.minimum(i, n_tiles - 1)
    seq_start = (tile * tm) % SEQ == 0
    prev_seq_start = ((i - 1) * tm) % SEQ == 0

    @pl.when(i == 0)
    def _():
        gates_ref[...] = jnp.zeros_like(gates_ref)
        m_ref[...] = jnp.zeros_like(m_ref)

    m_start = jnp.where(prev_seq_start, 0.0, m_ref[0:1, :])
    m_ref[0:1, :] = _gate_weights(gates_ref[...], m_start, wi_ref, cv_ref)

    halo = jnp.where(seq_start, 0.0, halo_ref[...].astype(F32))
    for blk in range(N_LANE_BLOCKS_A):
        sl = slice(blk * LANES, (blk + 1) * LANES)
        ext_ref[blk, pl.ds(0, SUBLANES, stride=2), :] = halo[:, sl]
        ext_ref[blk, pl.ds(2 * SUBLANES, tm, stride=2), :] = xm_ref[:, sl].astype(F32)
        acc = cb_ref[:, sl]
        for j in range(CONV_K):
            off = 2 * (SUBLANES - (CONV_K - 1) + j)
            acc = acc + cw_ref[j:j + 1, sl] * ext_ref[blk, pl.ds(off, tm, stride=2), :]
        xc_ref[:, sl] = (acc * _sigmoid(acc)).astype(BF16)
        qk = _dot(xc_ref[:, sl], bdqk_ref[blk])
        q_ref[:, sl] = qk[:, :LANES].astype(BF16)
        k_ref[:, sl] = qk[:, LANES:].astype(BF16)
        v_ref[:, sl] = _dot(xm_ref[:, sl], bdv_ref[blk]).astype(BF16)

    gates_ref[...] = _dot(xc_ref[...], wgc_ref[...]) + _dot(xm_ref[...], wgm_ref[...]) + bg_ref[...]


def _gate_weights(gates, m_start, wi_ref, cv_ref):
    tm = gates.shape[0]
    L = CHUNK_A
    scale = HEAD_DIM_A ** -0.5
    log_sig = jnp.minimum(gates, 0.0) - jnp.log(1.0 + jnp.exp(-jnp.abs(gates)))
    log_f = pltpu.roll(log_sig, LANES - HEADS_A, 1)
    row = lax.broadcasted_iota(jnp.int32, (L, L), 0)
    col = lax.broadcasted_iota(jnp.int32, (L, L), 1)
    tri = col <= row
    tri_b = jnp.where(tri, 1.0, 0.0).astype(BF16)
    lane = lax.broadcasted_iota(jnp.int32, (L, LANES), 1)
    head_lane = lax.broadcasted_iota(jnp.int32, (1, LANES), 1) < HEADS_A
    chunks = [slice(c * L, (c + 1) * L) for c in range(tm // L)]

    parts = []
    for rows in chunks:
        li = gates[rows, :]
        lf = log_f[rows, :]
        lf_hi = lf.astype(BF16)
        lf_mid = (lf - lf_hi.astype(F32)).astype(BF16)
        lf_lo = (lf - lf_hi.astype(F32) - lf_mid.astype(F32)).astype(BF16)
        b = _dot(tri_b, lf_hi) + _dot(tri_b, lf_mid) + _dot(tri_b, lf_lo)
        g_tot = b[L - 1:L, :]
        li_rows = li.T[0:SUBLANES, :]
        b_rows = b.T[0:SUBLANES, :]
        row_max = jnp.zeros((L, LANES), F32)
        for h in range(HEADS_A):
            d_intra = jnp.where(tri, b[:, h:h + 1] - b_rows[h:h + 1, :] + li_rows[h:h + 1, :], -jnp.inf)
            wi_ref[h, rows, :] = d_intra
            row_max = jnp.where(lane == h, jnp.max(d_intra, axis=1, keepdims=True), row_max)
        lw = g_tot - b + li
        parts.append((b, g_tot, lw, jnp.max(lw, axis=0, keepdims=True), row_max))

    m_list = [m_start]
    for b, g_tot, lw, lw_max, row_max in parts:
        m_list.append(jnp.where(head_lane, jnp.maximum(g_tot + m_list[-1], lw_max), 0.0))

    for c, rows in enumerate(chunks):
        b, g_tot, lw, lw_max, row_max = parts[c]
        m_prev, m_new = m_list[c], m_list[c + 1]
        a_inter = b + m_prev
        m_t = jnp.maximum(a_inter, row_max)
        for h in range(HEADS_A):
            wi_ref[h, rows, :] = jnp.exp(wi_ref[h, rows, :] - m_t[:, h:h + 1]) * scale
        w_inter = jnp.exp(a_inter - m_t)
        floor = jnp.exp(-m_t)
        w_state = jnp.exp(lw - m_new) * scale
        decay = jnp.broadcast_to(jnp.exp(g_tot + m_prev - m_new), (L, LANES))
        cv_ref[rows, :] = jnp.where(
            lane < CV_FLOOR, w_inter,
            jnp.where(lane < CV_W_STATE, pltpu.roll(floor, CV_FLOOR, 1),
                      jnp.where(lane < CV_DECAY, pltpu.roll(w_state, CV_W_STATE, 1),
                                jnp.where(lane < CV_DECAY + HEADS_A, pltpu.roll(decay, CV_DECAY, 1),
                                          0.0))))
    return m_list[-1]


def _mlstm_pre(proj, layer, cw, cb, bdqk, bdv, wgc, wgm, bg, name, tm=512):
    n_rows = proj.shape[0]
    n_tiles = n_rows // tm
    act = jax.ShapeDtypeStruct((n_rows, INNER_A), BF16)
    cur = lambda i: jnp.minimum(i, n_tiles - 1)
    prev = lambda i: jnp.maximum(i - 1, 0)
    row_spec = pl.BlockSpec((tm, INNER_A), lambda i: (cur(i), 0))
    full = lambda shape: pl.BlockSpec((None,) + shape, lambda i: (layer,) + (0,) * len(shape))
    return pl.pallas_call(
        functools.partial(_pre_kernel, n_tiles=n_tiles),
        grid=(n_tiles + 1,),
        in_specs=[row_spec,
                  pl.BlockSpec((SUBLANES, INNER_A),
                               lambda i: (jnp.maximum(cur(i) * (tm // SUBLANES) - 1, 0), 0)),
                  full((CONV_K, INNER_A)), full((1, INNER_A)),
                  full((N_LANE_BLOCKS_A, LANES, 2 * LANES)), full((N_LANE_BLOCKS_A, LANES, LANES)),
                  full((INNER_A, LANES)), full((INNER_A, LANES)), full((1, LANES))],
        out_specs=[row_spec, row_spec, row_spec, row_spec,
                   pl.BlockSpec((HEADS_A, tm, CHUNK_A), lambda i: (0, prev(i), 0)),
                   pl.BlockSpec((tm, LANES), lambda i: (prev(i), 0))],
        out_shape=[act, act, act, act,
                   jax.ShapeDtypeStruct((HEADS_A, n_rows, CHUNK_A), F32),
                   jax.ShapeDtypeStruct((n_rows, LANES), F32)],
        scratch_shapes=[pltpu.VMEM((N_LANE_BLOCKS_A, 2 * (tm + SUBLANES), LANES), F32),
                        pltpu.VMEM((SUBLANES, LANES), F32), pltpu.VMEM((tm, LANES), F32)],
        compiler_params=pltpu.CompilerParams(dimension_semantics=("arbitrary",),
                                             vmem_limit_bytes=VMEM_LIMIT),
        name=name,
    )(proj, proj, cw, cb, bdqk, bdv, wgc, wgm, bg)


def _mlstm_kernel(q_ref, k_ref, v_ref, xc_ref, z_ref, o_ref, wi_ref, cv_ref, gn_ref, sk_ref, y_ref,
                  ct_ref, n_ref):
    @pl.when(pl.program_id(1) == 0)
    def _():
        ct_ref[...] = jnp.zeros_like(ct_ref)
        n_ref[...] = jnp.zeros_like(n_ref)

    for c in range(q_ref.shape[0] // CHUNK_A):
        r = slice(c * CHUNK_A, (c + 1) * CHUNK_A)
        for h in range(HEADS_A):
            hs = slice(h * HEAD_DIM_A, (h + 1) * HEAD_DIM_A)
            qb = q_ref[r, hs]
            kb = k_ref[r, hs]
            vb = v_ref[r, hs]
            w_inter = cv_ref[r, CV_W_INTER + h:CV_W_INTER + h + 1]
            floor = cv_ref[r, CV_FLOOR + h:CV_FLOOR + h + 1]
            w_state = cv_ref[r, CV_W_STATE + h:CV_W_STATE + h + 1]
            decay = cv_ref[c * CHUNK_A:c * CHUNK_A + 1, CV_DECAY + h:CV_DECAY + h + 1]

            sc = _dot_nt(qb, kb) * wi_ref[h, r, :]
            num = w_inter * _dot(qb, ct_ref[h].astype(BF16)) + _dot(sc.astype(BF16), vb)
            n_rows = jnp.broadcast_to(n_ref[h].astype(BF16), (LANES, HEAD_DIM_A))
            den = w_inter * _dot_nt(qb, n_rows)[:, 0:1] + jnp.sum(sc, axis=1, keepdims=True)
            h_tilde = num * (1.0 / jnp.maximum(jnp.abs(den), floor))

            kw = kb * w_state.astype(BF16)
            ct_ref[h] = decay * ct_ref[h] + lax.dot_general(
                kw, vb, (((0,), (0,)), ((), ())), preferred_element_type=F32)
            n_ref[h] = decay * n_ref[h] + jnp.sum(kw.astype(F32), axis=0, keepdims=True)

            hg = h_tilde * _sigmoid(o_ref[r, hs].astype(F32))
            mu = jnp.mean(hg, axis=1, keepdims=True)
            dlt = hg - mu
            var = jnp.mean(dlt * dlt, axis=1, keepdims=True)
            hn = dlt * lax.rsqrt(var + LN_EPS) * gn_ref[:, hs]
            zz = z_ref[r, hs].astype(F32)
            y = (hn + sk_ref[:, hs] * xc_ref[r, hs].astype(F32)) * (zz * _sigmoid(zz))
            y_ref[r, hs] = y.astype(BF16)


def _mlstm_core(q, k, v, xc, proj, wi, cv, layer, gn, skip, name, chunks_per_step=2):
    L = chunks_per_step * CHUNK_A
    dh = HEAD_DIM_A
    steps = SEQ // L
    row = lambda b, c: b * steps + c
    act_spec = pl.BlockSpec((L, INNER_A), lambda b, c: (row(b, c), 0))
    vec_spec = pl.BlockSpec((None, 1, INNER_A), lambda b, c: (layer, 0, 0))
    return pl.pallas_call(
        _mlstm_kernel,
        grid=(BATCH, steps),
        in_specs=[act_spec, act_spec, act_spec, act_spec,
                  pl.BlockSpec((L, INNER_A), lambda b, c: (row(b, c), 1)),
                  pl.BlockSpec((L, INNER_A), lambda b, c: (row(b, c), 2)),
                  pl.BlockSpec((HEADS_A, L, CHUNK_A), lambda b, c: (0, row(b, c), 0)),
                  pl.BlockSpec((L, LANES), lambda b, c: (row(b, c), 0)),
                  vec_spec, vec_spec],
        out_specs=act_spec,
        out_shape=jax.ShapeDtypeStruct((TOKENS, INNER_A), BF16),
        scratch_shapes=[pltpu.VMEM((HEADS_A, dh, dh), F32), pltpu.VMEM((HEADS_A, 1, dh), F32)],
        compiler_params=pltpu.CompilerParams(
            dimension_semantics=("parallel", "arbitrary"), vmem_limit_bytes=VMEM_LIMIT),
        name=name,
    )(q, k, v, xc, proj, proj, wi, cv, gn, skip)


def _out_ln_kernel(y_ref, w_ref, x_ref, g_ref, b_ref, *rest, with_perm):
    n_out = 1 + (N_GROUPS_B if with_perm else 0)
    outs, scratch = rest[:n_out], rest[n_out:]
    wb_ref = scratch[0]

    @pl.when(pl.program_id(0) == 0)
    def _():
        wb_ref[...] = w_ref[...].astype(BF16)

    r = ALPHA * x_ref[...] + _dot(y_ref[...], wb_ref[...])
    slabs = scratch[1:] if with_perm else (None, None)
    _store_stream(_layer_norm_rows(r, g_ref[...], b_ref[...]), outs[0], outs[1:], *slabs)


def _out_ln(y, w, layer, x, g, b, ln_layer, name, with_perm, tm=512):
    m, k = y.shape
    n = w.shape[2]
    out_shape, out_specs = _stream_out(tm, with_perm)
    return pl.pallas_call(
        functools.partial(_out_ln_kernel, with_perm=with_perm),
        grid=(m // tm,),
        in_specs=[pl.BlockSpec((tm, k), lambda i: (i, 0)),
                  pl.BlockSpec((None, k, n), lambda i: (layer, 0, 0)),
                  pl.BlockSpec((tm, n), lambda i: (i, 0)),
                  pl.BlockSpec((None, 1, n), lambda i: (ln_layer, 0, 0)),
                  pl.BlockSpec((None, 1, n), lambda i: (ln_layer, 0, 0))],
        out_specs=out_specs,
        out_shape=out_shape,
        scratch_shapes=([pltpu.VMEM((k, n), BF16)]
                        + ([pltpu.VMEM((N_SLABS, tm, LANES), F32)] * 2 if with_perm else [])),
        compiler_params=pltpu.CompilerParams(dimension_semantics=("arbitrary",),
                                             vmem_limit_bytes=VMEM_LIMIT),
        name=name,
    )(y, w, x, g, b)


def _attn_kernel(q_ref, kvp_ref, kvc_ref, o_ref, lse_ref, *, dilation, blocks_per_seq):
    n = BLOCK_B
    n_blocks = q_ref.shape[0] // n
    qi = lax.broadcasted_iota(jnp.int32, (n, n), 0)
    kj = lax.broadcasted_iota(jnp.int32, (n, n), 1)
    lower = kj <= qi
    diag = kj == qi
    dist = jnp.bitwise_and(qi - kj, n - 1).astype(F32)
    heads = range(HEADS_B)
    ksl = [slice(h * HEAD_DIM_B, (h + 1) * HEAD_DIM_B) for h in heads]
    vsl = [slice(INNER_B + h * HEAD_DIM_B, INNER_B + (h + 1) * HEAD_DIM_B) for h in heads]
    slopes = [2.0 ** (-8.0 * (h + 1.0) / HEADS_B) * dilation for h in heads]

    for blk in range(n_blocks):
        rows = slice(blk * n, (blk + 1) * n)
        prev_ref, prev_rows = ((kvp_ref, slice(0, n)) if blk == 0
                               else (kvc_ref, slice((blk - 1) * n, blk * n)))
        first = (pl.program_id(0) * n_blocks + blk) % blocks_per_seq == 0
        prev_bias = jnp.where(first, -jnp.inf, 0.0)

        both_rows = slice((blk - 1) * n, (blk + 1) * n)
        scores = []
        for h in heads:
            qh = q_ref[rows, ksl[h]]
            if blk == 0:
                s_prev = _dot_nt(qh, prev_ref[prev_rows, ksl[h]])
                s_cur = _dot_nt(qh, kvc_ref[rows, ksl[h]])
            else:
                s_both = _dot_nt(qh, kvc_ref[both_rows, ksl[h]])
                s_prev, s_cur = s_both[:, :n], s_both[:, n:]
            s_prev = s_prev + prev_bias
            s = jnp.where(lower, s_cur, s_prev) - slopes[h] * dist
            s_diag = (jnp.max(jnp.where(diag, s_prev, -jnp.inf), axis=-1, keepdims=True)
                      - slopes[h] * float(n))
            m = jnp.maximum(jnp.max(s, axis=-1, keepdims=True), s_diag)
            scores.append((s, s_diag, m))
        probs = []
        for h in heads:
            s, s_diag, m = scores[h]
            e = jnp.exp(s - m)
            e_diag = jnp.exp(s_diag - m)
            den = jnp.sum(e, axis=-1, keepdims=True) + e_diag
            p_prev = jnp.where(lower, jnp.where(diag, e_diag, 0.0), e)
            probs.append((jnp.where(lower, e, 0.0).astype(BF16), p_prev.astype(BF16), den))
        stats = jnp.ones((n, LANES), F32)
        for h in heads:
            p_cur, p_prev, den = probs[h]
            if blk == 0:
                o = _dot(p_cur, kvc_ref[rows, vsl[h]]) + _dot(p_prev, prev_ref[prev_rows, vsl[h]])
            else:
                o = _dot(jnp.concatenate([p_prev, p_cur], axis=1), kvc_ref[both_rows, vsl[h]])
            o_ref[rows, ksl[h]] = o.astype(BF16)
            stats = jnp.where(kj == h, den, jnp.where(kj == HEADS_B + h, scores[h][2], stats))
        lse_ref[rows, :] = stats


def _attn_group(q, kv, group, name, blocks_per_step=8):
    dilation = DILATIONS[group]
    blocks_per_seq = SEQ // dilation // BLOCK_B
    tq = blocks_per_step * BLOCK_B
    return pl.pallas_call(
        functools.partial(_attn_kernel, dilation=float(dilation), blocks_per_seq=blocks_per_seq),
        grid=(TOKENS // tq,),
        in_specs=[pl.BlockSpec((tq, INNER_B), lambda n: (n, 0)),
                  pl.BlockSpec((BLOCK_B, 2 * INNER_B),
                               lambda n: (jnp.maximum(n * blocks_per_step - 1, 0), 0)),
                  pl.BlockSpec((tq, 2 * INNER_B), lambda n: (n, 0))],
        out_specs=[pl.BlockSpec((tq, INNER_B), lambda n: (n, 0)),
                   pl.BlockSpec((tq, LANES), lambda n: (n, 0))],
        out_shape=[jax.ShapeDtypeStruct((TOKENS, INNER_B), BF16),
                   jax.ShapeDtypeStruct((TOKENS, LANES), F32)],
        compiler_params=pltpu.CompilerParams(dimension_semantics=("arbitrary",),
                                             vmem_limit_bytes=VMEM_LIMIT),
        name=name,
    )(q, kv, kv)


def _merge_out_ln_kernel(o0_ref, o1_ref, o2_ref, l0_ref, l1_ref, l2_ref, z_ref, w_ref, x_ref,
                         g_ref, b_ref, *rest, with_perm):
    n_out = 1 + (N_GROUPS_B if with_perm else 0)
    outs = rest[:n_out]
    y_ref, os1_ref, os2_ref, tmp_ref, ls1_ref, ls2_ref, ltmp_ref, wb_ref = rest[n_out:]
    tm = x_ref.shape[0]

    @pl.when(pl.program_id(0) == 0)
    def _():
        wb_ref[...] = w_ref[...].astype(BF16)

    q4, q16 = tm // RES_STEP, tm // (RES_STEP * RES_STEP)
    for r4 in range(RES_STEP):
        ls1_ref[pl.ds(r4, q4, stride=RES_STEP), :] = l1_ref[r4]
        for a in range(RES_STEP):
            ltmp_ref[pl.ds(r4 * q4 + a, q16, stride=RES_STEP), :] = l2_ref[r4 + RES_STEP * a]
        ls2_ref[pl.ds(r4, q4, stride=RES_STEP), :] = ltmp_ref[r4 * q4:(r4 + 1) * q4, :]
    for s in range(N_SLABS):
        lanes = slice(s * LANES, (s + 1) * LANES)
        for r4 in range(RES_STEP):
            os1_ref[s, pl.ds(r4, q4, stride=RES_STEP), :] = o1_ref[r4, :, lanes].astype(F32)
            for a in range(RES_STEP):
                tmp_ref[s, pl.ds(r4 * q4 + a, q16, stride=RES_STEP), :] = (
                    o2_ref[r4 + RES_STEP * a, :, lanes].astype(F32))
            os2_ref[s, pl.ds(r4, q4, stride=RES_STEP), :] = tmp_ref[s, r4 * q4:(r4 + 1) * q4, :]
    stats = (l0_ref[...], ls1_ref[...], ls2_ref[...])
    maxes = [pltpu.roll(t, LANES - HEADS_B, 1) for t in stats]
    head_lane = lax.broadcasted_iota(jnp.int32, (1, LANES), 1) < HEADS_B
    lses = [m + jnp.log(jnp.where(head_lane, t, 1.0)) for m, t in zip(maxes, stats)]
    mx = jnp.maximum(jnp.maximum(lses[0], lses[1]), lses[2])
    inv = 1.0 / (jnp.exp(lses[0] - mx) + jnp.exp(lses[1] - mx) + jnp.exp(lses[2] - mx))
    w0, w1, w2 = [jnp.exp(m - mx) * inv for m in maxes]
    for h in range(HEADS_B):
        sl = slice(h * HEAD_DIM_B, (h + 1) * HEAD_DIM_B)
        o = (w0[:, h:h + 1] * o0_ref[:, sl].astype(F32) + w1[:, h:h + 1] * os1_ref[h]
             + w2[:, h:h + 1] * os2_ref[h])
        zz = z_ref[:, sl].astype(F32)
        y_ref[:, sl] = (o * (zz * _sigmoid(zz))).astype(BF16)
    r = ALPHA * x_ref[...] + _dot(y_ref[...], wb_ref[...])
    _store_stream(_layer_norm_rows(r, g_ref[...], b_ref[...]), outs[0], outs[1:], os1_ref, os2_ref)


def _merge_out_ln(os_, lses, z, w, layer, x, g, b, ln_layer, name, with_perm, tm=512):
    m = x.shape[0]
    n = D_MODEL
    per_seq = SEQ // tm
    tok = lambda width: pl.BlockSpec((tm, width), lambda i: (i, 0))
    res = lambda d, width: pl.BlockSpec((None, d, tm // d, width),
                                        lambda i: (i // per_seq, 0, i % per_seq, 0))
    vec = pl.BlockSpec((None, 1, n), lambda i: (ln_layer, 0, 0))
    d1, d2 = DILATIONS[1], DILATIONS[2]
    out_shape, out_specs = _stream_out(tm, with_perm)
    return pl.pallas_call(
        functools.partial(_merge_out_ln_kernel, with_perm=with_perm),
        grid=(m // tm,),
        in_specs=[tok(INNER_B), res(d1, INNER_B), res(d2, INNER_B),
                  tok(LANES), res(d1, LANES), res(d2, LANES),
                  tok(INNER_B),
                  pl.BlockSpec((None, INNER_B, n), lambda i: (layer, 0, 0)),
                  tok(n), vec, vec],
        out_specs=out_specs,
        out_shape=out_shape,
        scratch_shapes=[pltpu.VMEM((tm, INNER_B), BF16),
                        pltpu.VMEM((N_SLABS, tm, LANES), F32), pltpu.VMEM((N_SLABS, tm, LANES), F32),
                        pltpu.VMEM((N_SLABS, tm, LANES), F32),
                        pltpu.VMEM((tm, LANES), F32), pltpu.VMEM((tm, LANES), F32),
                        pltpu.VMEM((tm, LANES), F32), pltpu.VMEM((INNER_B, n), BF16)],
        compiler_params=pltpu.CompilerParams(dimension_semantics=("arbitrary",),
                                             vmem_limit_bytes=VMEM_LIMIT),
        name=name,
    )(os_[0], os_[1].reshape(BATCH, d1, SEQ // d1, INNER_B), os_[2].reshape(BATCH, d2, SEQ // d2, INNER_B),
      lses[0], lses[1].reshape(BATCH, d1, SEQ // d1, LANES), lses[2].reshape(BATCH, d2, SEQ // d2, LANES),
      z, w, x, g, b)


def _block_diag_lane_blocks(w):
    per = LANES // QKV_BLOCK
    w4 = w.reshape(-1, per, QKV_BLOCK, QKV_BLOCK)
    eye = jnp.eye(per, dtype=w.dtype)
    dense = jnp.einsum('bmij,mp->bmjpi', w4, eye)
    return dense.reshape(-1, LANES, LANES)


def _fold_gate_weights(wq, wk, wv, w_if):
    nblk = INNER_A // QKV_BLOCK
    wif = w_if.reshape(3, nblk, QKV_BLOCK, 2 * HEADS_A)
    hp = lax.Precision.HIGHEST
    wgc = (jnp.einsum('nij,nio->njo', wq, wif[0], precision=hp)
           + jnp.einsum('nij,nio->njo', wk, wif[1], precision=hp)).reshape(INNER_A, 2 * HEADS_A)
    wgm = jnp.einsum('nij,nio->njo', wv, wif[2], precision=hp).reshape(INNER_A, 2 * HEADS_A)
    pad = ((0, 0), (0, LANES - 2 * HEADS_A))
    return jnp.pad(wgc, pad).astype(BF16), jnp.pad(wgm, pad).astype(BF16)


def kernel(x, ln_g, ln_b, a_w_in, a_conv_w, a_conv_b, a_wq, a_wk, a_wv, a_w_if, a_b_if, a_gn_g, a_skip,
           a_w_out, b_w_kv, b_w_in, b_w_out):
    xs = x.reshape(TOKENS, D_MODEL)
    x_in = xs
    block_diag = jax.vmap(_block_diag_lane_blocks)
    bdqk = jnp.concatenate([block_diag(a_wq), block_diag(a_wk)], axis=-1).astype(BF16)
    bdv = block_diag(a_wv).astype(BF16)
    wgc, wgm = jax.vmap(_fold_gate_weights)(a_wq, a_wk, a_wv, a_w_if)
    bg = jnp.pad(a_b_if[:, None, :], ((0, 0), (0, 0), (0, LANES - 2 * HEADS_A)))
    conv_b, gn, skip = a_conv_b[:, None, :], a_gn_g[:, None, :], a_skip[:, None, :]
    ln_g3, ln_b3 = ln_g[:, None, :], ln_b[:, None, :]
    for layer in range(N_A_LAYERS):
        proj = _proj(x_in, a_w_in, layer, 0, 3 * INNER_A, f"a{layer}_proj")
        q, k, v, xc, wi, cv = _mlstm_pre(proj, layer, a_conv_w, conv_b, bdqk, bdv, wgc, wgm, bg,
                                         f"a{layer}_pre")
        y = _mlstm_core(q, k, v, xc, proj, wi, cv, layer, gn, skip, f"a{layer}_mlstm")
        outs = _out_ln(y, a_w_out, layer, xs, ln_g3, ln_b3, layer, f"a{layer}_out_ln",
                       with_perm=layer == N_A_LAYERS - 1)
        xs = outs[0]
        x_in = xs

    perms = [p.reshape(TOKENS, D_MODEL) for p in outs[1:]]
    kvs = [_proj(perms[g], b_w_kv[None], 0, 2 * g * INNER_B, 2 * INNER_B, f"kv_proj{g}")
           for g in range(N_GROUPS_B)]
    q_scale = HEAD_DIM_B ** -0.5
    for lb in range(N_B_LAYERS):
        layer = N_A_LAYERS + lb
        qs = [_proj(perms[g], b_w_in, lb, g * INNER_B, INNER_B, f"b{lb}_proj{g}", out_scale=q_scale)
              for g in range(N_GROUPS_B)]
        z = _proj(perms[0], b_w_in, lb, N_GROUPS_B * INNER_B, INNER_B, f"b{lb}_projz")
        os_, lses = [], []
        for g in range(N_GROUPS_B):
            o, lse = _attn_group(qs[g], kvs[g], g, f"b{lb}_attn{g}")
            os_.append(o)
            lses.append(lse)
        outs = _merge_out_ln(os_, lses, z, b_w_out, lb, xs, ln_g3, ln_b3, layer,
                             f"b{lb}_merge_out_ln", with_perm=lb < N_B_LAYERS - 1)
        xs = outs[0]
        perms = [p.reshape(TOKENS, D_MODEL) for p in outs[1:]]
    return xs.reshape(BATCH, SEQ, D_MODEL)
```

```python
import functools

import jax
import jax.numpy as jnp
from jax import lax
from jax.experimental import pallas as pl
from jax.experimental.pallas import tpu as pltpu

D_MODEL = 1024
BATCH = 4
SEQ = 4096
DEPTH = 4
N_A_LAYERS = DEPTH // 2
N_B_LAYERS = DEPTH - N_A_LAYERS
INNER_A = 2 * D_MODEL
HEADS_A = 4
HEAD_DIM_A = INNER_A // HEADS_A
QKV_BLOCK = 4
CONV_K = 4
HEAD_DIM_B = 128
HEADS_B = D_MODEL // HEAD_DIM_B
INNER_B = HEADS_B * HEAD_DIM_B
GROUPS_B = ((128, 1), (512, 4), (2048, 16))
N_GROUPS_B = len(GROUPS_B)
DILATIONS = tuple(d for _, d in GROUPS_B)
BLOCK_B = 128
RES_STEP = 4
assert DILATIONS == (1, RES_STEP, RES_STEP * RES_STEP)
ALPHA = (2 * DEPTH) ** 0.25
LN_EPS = 1e-5

TOKENS = BATCH * SEQ
LANES = 128
SUBLANES = 8
N_SLABS = D_MODEL // LANES
CHUNK_A = 256
N_LANE_BLOCKS_A = INNER_A // LANES
VMEM_LIMIT = 48 * 1024 * 1024
PROJ_TN = 1024
CV_W_INTER, CV_FLOOR, CV_W_STATE, CV_DECAY = 0, HEADS_A, 2 * HEADS_A, 3 * HEADS_A

F32 = jnp.float32
BF16 = jnp.bfloat16

assert all(w // d == BLOCK_B for w, d in GROUPS_B)


def _dot(a, b):
    return jnp.dot(a, b, preferred_element_type=F32)


def _dot_nt(a, b):
    return lax.dot_general(a, b, (((1,), (1,)), ((), ())), preferred_element_type=F32)


def _sigmoid(x):
    return 0.5 * jnp.tanh(0.5 * x) + 0.5


def _silu(x):
    h = 0.5 * x
    return h * jnp.tanh(h) + h


def _layer_norm_rows(r, g, b):
    mu = jnp.mean(r, axis=-1, keepdims=True)
    d = r - mu
    var = jnp.mean(d * d, axis=-1, keepdims=True)
    return d * lax.rsqrt(var + LN_EPS) * g + b


def _store_stream(r, x_ref, perm_refs, slab_ref, slab4_ref):
    x_ref[...] = r
    if not perm_refs:
        return
    xb1_ref, xb4_ref, xb16_ref = perm_refs
    xb1_ref[...] = r.astype(BF16)
    tm = r.shape[0]
    q4, q16 = tm // RES_STEP, tm // (RES_STEP * RES_STEP)
    for s in range(N_SLABS):
        lanes = slice(s * LANES, (s + 1) * LANES)
        slab_ref[s] = r[:, lanes]
        for r4 in range(RES_STEP):
            part = slab_ref[s, pl.ds(r4, q4, stride=RES_STEP), :]
            slab4_ref[s, r4 * q4:(r4 + 1) * q4, :] = part
            xb4_ref[r4, :, lanes] = part.astype(BF16)
        for r4 in range(RES_STEP):
            for a in range(RES_STEP):
                xb16_ref[r4 + RES_STEP * a, :, lanes] = (
                    slab4_ref[s, pl.ds(r4 * q4 + a, q16, stride=RES_STEP), :].astype(BF16))


def _stream_out(tm, with_perm):
    per_seq = SEQ // tm
    shapes = [jax.ShapeDtypeStruct((TOKENS, D_MODEL), F32)]
    specs = [pl.BlockSpec((tm, D_MODEL), lambda i: (i, 0))]
    if with_perm:
        shapes.append(jax.ShapeDtypeStruct((TOKENS, D_MODEL), BF16))
        specs.append(pl.BlockSpec((tm, D_MODEL), lambda i: (i, 0)))
        for d in DILATIONS[1:]:
            shapes.append(jax.ShapeDtypeStruct((BATCH, d, SEQ // d, D_MODEL), BF16))
            specs.append(pl.BlockSpec((None, d, tm // d, D_MODEL),
                                      lambda i: (i // per_seq, 0, i % per_seq, 0)))
    return shapes, specs


def _proj_kernel(x_ref, w_ref, o_ref, wb_ref, *, out_scale):
    @pl.when(pl.program_id(1) == 0)
    def _():
        wb_ref[...] = w_ref[...].astype(BF16)

    acc = _dot(x_ref[...].astype(BF16), wb_ref[...])
    if out_scale is not None:
        acc = acc * out_scale
    o_ref[...] = acc.astype(o_ref.dtype)


def _proj(x, w, layer, col0, n, name, out_scale=None, tm=2048, tn=PROJ_TN):
    m, k = x.shape
    col_blk0 = col0 // tn
    return pl.pallas_call(
        functools.partial(_proj_kernel, out_scale=out_scale),
        grid=(n // tn, m // tm),
        in_specs=[pl.BlockSpec((tm, k), lambda j, i: (i, 0)),
                  pl.BlockSpec((None, k, tn), lambda j, i: (layer, 0, col_blk0 + j))],
        out_specs=pl.BlockSpec((tm, tn), lambda j, i: (i, j)),
        out_shape=jax.ShapeDtypeStruct((m, n), BF16),
        scratch_shapes=[pltpu.VMEM((k, tn), BF16)],
        compiler_params=pltpu.CompilerParams(dimension_semantics=("parallel", "arbitrary"),
                                             vmem_limit_bytes=VMEM_LIMIT),
        name=name,
    )(x, w)


def _pre_kernel(xm_ref, halo_ref, cw_ref, cb_ref, bdqk_ref, bdv_ref, wgc_ref, wgm_ref, bg_ref,
                q_ref, k_ref, v_ref, xc_ref, wi_ref, cv_ref, ext_ref, m_ref, gates_ref, *, n_tiles):
    tm = xm_ref.shape[0]
    i = pl.program_id(0)
    tile = jnp.minimum(i, n_tiles - 1)
    seq_start = (tile * tm) % SEQ == 0
    prev_seq_start = ((i - 1) * tm) % SEQ == 0

    @pl.when(i == 0)
    def _():
        gates_ref[...] = jnp.zeros_like(gates_ref)
        m_ref[...] = jnp.zeros_like(m_ref)

    m_start = jnp.where(prev_seq_start, 0.0, m_ref[0:1, :])
    m_ref[0:1, :] = _gate_weights(gates_ref[...], m_start, wi_ref, cv_ref)

    halo = jnp.where(seq_start, 0.0, halo_ref[...].astype(F32))
    for blk in range(N_LANE_BLOCKS_A):
        sl = slice(blk * LANES, (blk + 1) * LANES)
        ext_ref[blk, pl.ds(0, SUBLANES, stride=2), :] = halo[:, sl]
        ext_ref[blk, pl.ds(2 * SUBLANES, tm, stride=2), :] = xm_ref[:, sl].astype(F32)
        acc = cb_ref[:, sl]
        for j in range(CONV_K):
            off = 2 * (SUBLANES - (CONV_K - 1) + j)
            acc = acc + cw_ref[j:j + 1, sl] * ext_ref[blk, pl.ds(off, tm, stride=2), :]
        xc_ref[:, sl] = _silu(acc).astype(BF16)
        qk = _dot(xc_ref[:, sl], bdqk_ref[blk])
        q_ref[:, sl] = qk[:, :LANES].astype(BF16)
        k_ref[:, sl] = qk[:, LANES:].astype(BF16)
        v_ref[:, sl] = _dot(xm_ref[:, sl], bdv_ref[blk]).astype(BF16)

    gates_ref[...] = _dot(xc_ref[...], wgc_ref[...]) + _dot(xm_ref[...], wgm_ref[...]) + bg_ref[...]


def _gate_weights(gates, m_start, wi_ref, cv_ref):
    tm = gates.shape[0]
    L = CHUNK_A
    scale = HEAD_DIM_A ** -0.5
    log_sig = jnp.minimum(gates, 0.0) - jnp.log(1.0 + jnp.exp(-jnp.abs(gates)))
    log_f = pltpu.roll(log_sig, LANES - HEADS_A, 1)
    row = lax.broadcasted_iota(jnp.int32, (L, L), 0)
    col = lax.broadcasted_iota(jnp.int32, (L, L), 1)
    tri = col <= row
    tri_b = jnp.where(tri, 1.0, 0.0).astype(BF16)
    lane = lax.broadcasted_iota(jnp.int32, (L, LANES), 1)
    head_lane = lax.broadcasted_iota(jnp.int32, (1, LANES), 1) < HEADS_A
    chunks = [slice(c * L, (c + 1) * L) for c in range(tm // L)]

    parts = []
    for rows in chunks:
        li = gates[rows, :]
        lf = log_f[rows, :]
        lf_hi = lf.astype(BF16)
        lf_mid = (lf - lf_hi.astype(F32)).astype(BF16)
        lf_lo = (lf - lf_hi.astype(F32) - lf_mid.astype(F32)).astype(BF16)
        b = _dot(tri_b, lf_hi) + _dot(tri_b, lf_mid) + _dot(tri_b, lf_lo)
        g_tot = b[L - 1:L, :]
        li_rows = li.T[0:SUBLANES, :]
        b_rows = b.T[0:SUBLANES, :]
        row_max = jnp.zeros((L, LANES), F32)
        for h in range(HEADS_A):
            d_intra = jnp.where(tri, b[:, h:h + 1] - b_rows[h:h + 1, :] + li_rows[h:h + 1, :], -jnp.inf)
            wi_ref[h, rows, :] = d_intra
            row_max = jnp.where(lane == h, jnp.max(d_intra, axis=1, keepdims=True), row_max)
        lw = g_tot - b + li
        parts.append((b, g_tot, lw, jnp.max(lw, axis=0, keepdims=True), row_max))

    m_list = [m_start]
    for b, g_tot, lw, lw_max, row_max in parts:
        m_list.append(jnp.where(head_lane, jnp.maximum(g_tot + m_list[-1], lw_max), 0.0))

    for c, rows in enumerate(chunks):
        b, g_tot, lw, lw_max, row_max = parts[c]
        m_prev, m_new = m_list[c], m_list[c + 1]
        a_inter = b + m_prev
        m_t = jnp.maximum(a_inter, row_max)
        for h in range(HEADS_A):
            wi_ref[h, rows, :] = jnp.exp(wi_ref[h, rows, :] - m_t[:, h:h + 1]) * scale
        w_inter = jnp.exp(a_inter - m_t)
        floor = jnp.exp(-m_t)
        w_state = jnp.exp(lw - m_new) * scale
        decay = jnp.broadcast_to(jnp.exp(g_tot + m_prev - m_new), (L, LANES))
        cv_ref[rows, :] = jnp.where(
            lane < CV_FLOOR, w_inter,
            jnp.where(lane < CV_W_STATE, pltpu.roll(floor, CV_FLOOR, 1),
                      jnp.where(lane < CV_DECAY, pltpu.roll(w_state, CV_W_STATE, 1),
                                jnp.where(lane < CV_DECAY + HEADS_A, pltpu.roll(decay, CV_DECAY, 1),
                                          0.0))))
    return m_list[-1]


def _mlstm_pre(proj, layer, cw, cb, bdqk, bdv, wgc, wgm, bg, name, tm=512):
    n_rows = proj.shape[0]
    n_tiles = n_rows // tm
    act = jax.ShapeDtypeStruct((n_rows, INNER_A), BF16)
    cur = lambda i: jnp.minimum(i, n_tiles - 1)
    prev = lambda i: jnp.maximum(i - 1, 0)
    row_spec = pl.BlockSpec((tm, INNER_A), lambda i: (cur(i), 0))
    full = lambda shape: pl.BlockSpec((None,) + shape, lambda i: (layer,) + (0,) * len(shape))
    return pl.pallas_call(
        functools.partial(_pre_kernel, n_tiles=n_tiles),
        grid=(n_tiles + 1,),
        in_specs=[row_spec,
                  pl.BlockSpec((SUBLANES, INNER_A),
                               lambda i: (jnp.maximum(cur(i) * (tm // SUBLANES) - 1, 0), 0)),
                  full((CONV_K, INNER_A)), full((1, INNER_A)),
                  full((N_LANE_BLOCKS_A, LANES, 2 * LANES)), full((N_LANE_BLOCKS_A, LANES, LANES)),
                  full((INNER_A, LANES)), full((INNER_A, LANES)), full((1, LANES))],
        out_specs=[row_spec, row_spec, row_spec, row_spec,
                   pl.BlockSpec((HEADS_A, tm, CHUNK_A), lambda i: (0, prev(i), 0)),
                   pl.BlockSpec((tm, LANES), lambda i: (prev(i), 0))],
        out_shape=[act, act, act, act,
                   jax.ShapeDtypeStruct((HEADS_A, n_rows, CHUNK_A), F32),
                   jax.ShapeDtypeStruct((n_rows, LANES), F32)],
        scratch_shapes=[pltpu.VMEM((N_LANE_BLOCKS_A, 2 * (tm + SUBLANES), LANES), F32),
                        pltpu.VMEM((SUBLANES, LANES), F32), pltpu.VMEM((tm, LANES), F32)],
        compiler_params=pltpu.CompilerParams(dimension_semantics=("arbitrary",),
                                             vmem_limit_bytes=VMEM_LIMIT),
        name=name,
    )(proj, proj, cw, cb, bdqk, bdv, wgc, wgm, bg)


def _mlstm_kernel(q_ref, k_ref, v_ref, xc_ref, z_ref, o_ref, wi_ref, cv_ref, gn_ref, sk_ref, y_ref,
                  ct_ref, n_ref):
    @pl.when(pl.program_id(1) == 0)
    def _():
        ct_ref[...] = jnp.zeros_like(ct_ref)
        n_ref[...] = jnp.zeros_like(n_ref)

    for c in range(q_ref.shape[0] // CHUNK_A):
        r = slice(c * CHUNK_A, (c + 1) * CHUNK_A)
        for h in range(HEADS_A):
            hs = slice(h * HEAD_DIM_A, (h + 1) * HEAD_DIM_A)
            qb = q_ref[r, hs]
            kb = k_ref[r, hs]
            vb = v_ref[r, hs]
            w_inter = cv_ref[r, CV_W_INTER + h:CV_W_INTER + h + 1]
            floor = cv_ref[r, CV_FLOOR + h:CV_FLOOR + h + 1]
            w_state = cv_ref[r, CV_W_STATE + h:CV_W_STATE + h + 1]
            decay = cv_ref[c * CHUNK_A:c * CHUNK_A + 1, CV_DECAY + h:CV_DECAY + h + 1]

            sc = _dot_nt(qb, kb) * wi_ref[h, r, :]
            num = w_inter * _dot(qb, ct_ref[h].astype(BF16)) + _dot(sc.astype(BF16), vb)
            n_rows = jnp.broadcast_to(n_ref[h].astype(BF16), (LANES, HEAD_DIM_A))
            den = w_inter * _dot_nt(qb, n_rows)[:, 0:1] + jnp.sum(sc, axis=1, keepdims=True)
            h_tilde = num * (1.0 / jnp.maximum(jnp.abs(den), floor))

            kw = kb * w_state.astype(BF16)
            ct_ref[h] = decay * ct_ref[h] + lax.dot_general(
                kw, vb, (((0,), (0,)), ((), ())), preferred_element_type=F32)
            n_ref[h] = decay * n_ref[h] + jnp.sum(kw.astype(F32), axis=0, keepdims=True)

            hg = h_tilde * _sigmoid(o_ref[r, hs].astype(F32))
            mu = jnp.mean(hg, axis=1, keepdims=True)
            dlt = hg - mu
            var = jnp.mean(dlt * dlt, axis=1, keepdims=True)
            hn = dlt * lax.rsqrt(var + LN_EPS) * gn_ref[:, hs]
            zz = z_ref[r, hs].astype(F32)
            y = (hn + sk_ref[:, hs] * xc_ref[r, hs].astype(F32)) * _silu(zz)
            y_ref[r, hs] = y.astype(BF16)


def _mlstm_core(q, k, v, xc, proj, wi, cv, layer, gn, skip, name, chunks_per_step=2):
    L = chunks_per_step * CHUNK_A
    dh = HEAD_DIM_A
    steps = SEQ // L
    row = lambda b, c: b * steps + c
    act_spec = pl.BlockSpec((L, INNER_A), lambda b, c: (row(b, c), 0))
    vec_spec = pl.BlockSpec((None, 1, INNER_A), lambda b, c: (layer, 0, 0))
    return pl.pallas_call(
        _mlstm_kernel,
        grid=(BATCH, steps),
        in_specs=[act_spec, act_spec, act_spec, act_spec,
                  pl.BlockSpec((L, INNER_A), lambda b, c: (row(b, c), 1)),
                  pl.BlockSpec((L, INNER_A), lambda b, c: (row(b, c), 2)),
                  pl.BlockSpec((HEADS_A, L, CHUNK_A), lambda b, c: (0, row(b, c), 0)),
                  pl.BlockSpec((L, LANES), lambda b, c: (row(b, c), 0)),
                  vec_spec, vec_spec],
        out_specs=act_spec,
        out_shape=jax.ShapeDtypeStruct((TOKENS, INNER_A), BF16),
        scratch_shapes=[pltpu.VMEM((HEADS_A, dh, dh), F32), pltpu.VMEM((HEADS_A, 1, dh), F32)],
        compiler_params=pltpu.CompilerParams(
            dimension_semantics=("parallel", "arbitrary"), vmem_limit_bytes=VMEM_LIMIT),
        name=name,
    )(q, k, v, xc, proj, proj, wi, cv, gn, skip)


def _out_ln_kernel(y_ref, w_ref, x_ref, g_ref, b_ref, *rest, with_perm):
    n_out = 1 + (N_GROUPS_B if with_perm else 0)
    outs, scratch = rest[:n_out], rest[n_out:]
    wb_ref = scratch[0]

    @pl.when(pl.program_id(0) == 0)
    def _():
        wb_ref[...] = w_ref[...].astype(BF16)

    r = ALPHA * x_ref[...] + _dot(y_ref[...], wb_ref[...])
    slabs = scratch[1:] if with_perm else (None, None)
    _store_stream(_layer_norm_rows(r, g_ref[...], b_ref[...]), outs[0], outs[1:], *slabs)


def _out_ln(y, w, layer, x, g, b, ln_layer, name, with_perm, tm=512):
    m, k = y.shape
    n = w.shape[2]
    out_shape, out_specs = _stream_out(tm, with_perm)
    return pl.pallas_call(
        functools.partial(_out_ln_kernel, with_perm=with_perm),
        grid=(m // tm,),
        in_specs=[pl.BlockSpec((tm, k), lambda i: (i, 0)),
                  pl.BlockSpec((None, k, n), lambda i: (layer, 0, 0)),
                  pl.BlockSpec((tm, n), lambda i: (i, 0)),
                  pl.BlockSpec((None, 1, n), lambda i: (ln_layer, 0, 0)),
                  pl.BlockSpec((None, 1, n), lambda i: (ln_layer, 0, 0))],
        out_specs=out_specs,
        out_shape=out_shape,
        scratch_shapes=([pltpu.VMEM((k, n), BF16)]
                        + ([pltpu.VMEM((N_SLABS, tm, LANES), F32)] * 2 if with_perm else [])),
        compiler_params=pltpu.CompilerParams(dimension_semantics=("arbitrary",),
                                             vmem_limit_bytes=VMEM_LIMIT),
        name=name,
    )(y, w, x, g, b)


def _attn_kernel(q_ref, kvp_ref, kvc_ref, o_ref, lse_ref, *, dilation, blocks_per_seq):
    n = BLOCK_B
    n_blocks = q_ref.shape[0] // n
    qi = lax.broadcasted_iota(jnp.int32, (n, n), 0)
    kj = lax.broadcasted_iota(jnp.int32, (n, n), 1)
    lower = kj <= qi
    diag = kj == qi
    dist = jnp.bitwise_and(qi - kj, n - 1).astype(F32)
    heads = range(HEADS_B)
    ksl = [slice(h * HEAD_DIM_B, (h + 1) * HEAD_DIM_B) for h in heads]
    vsl = [slice(INNER_B + h * HEAD_DIM_B, INNER_B + (h + 1) * HEAD_DIM_B) for h in heads]
    slopes = [2.0 ** (-8.0 * (h + 1.0) / HEADS_B) * dilation for h in heads]

    for blk in range(n_blocks):
        rows = slice(blk * n, (blk + 1) * n)
        prev_ref, prev_rows = ((kvp_ref, slice(0, n)) if blk == 0
                               else (kvc_ref, slice((blk - 1) * n, blk * n)))
        first = (pl.program_id(0) * n_blocks + blk) % blocks_per_seq == 0
        prev_bias = jnp.where(first, -jnp.inf, 0.0)

        both_rows = slice((blk - 1) * n, (blk + 1) * n)
        scores = []
        for h in heads:
            qh = q_ref[rows, ksl[h]]
            if blk == 0:
                s_prev = _dot_nt(qh, prev_ref[prev_rows, ksl[h]])
                s_cur = _dot_nt(qh, kvc_ref[rows, ksl[h]])
            else:
                s_both = _dot_nt(qh, kvc_ref[both_rows, ksl[h]])
                s_prev, s_cur = s_both[:, :n], s_both[:, n:]
            s_prev = s_prev + prev_bias
            s = jnp.where(lower, s_cur, s_prev) - slopes[h] * dist
            s_diag = (jnp.max(jnp.where(diag, s_prev, -jnp.inf), axis=-1, keepdims=True)
                      - slopes[h] * float(n))
            m = jnp.maximum(jnp.max(s, axis=-1, keepdims=True), s_diag)
            scores.append((s, s_diag, m))
        probs = []
        for h in heads:
            s, s_diag, m = scores[h]
            e = jnp.exp(s - m)
            e_diag = jnp.exp(s_diag - m)
            den = jnp.sum(e, axis=-1, keepdims=True) + e_diag
            p_prev = jnp.where(lower, jnp.where(diag, e_diag, 0.0), e)
            probs.append((jnp.where(lower, e, 0.0).astype(BF16), p_prev.astype(BF16), den))
        stats = jnp.ones((n, LANES), F32)
        for h in heads:
            p_cur, p_prev, den = probs[h]
            if blk == 0:
                o = _dot(p_cur, kvc_ref[rows, vsl[h]]) + _dot(p_prev, prev_ref[prev_rows, vsl[h]])
            else:
                o = _dot(jnp.concatenate([p_prev, p_cur], axis=1), kvc_ref[both_rows, vsl[h]])
            o_ref[rows, ksl[h]] = o.astype(BF16)
            stats = jnp.where(kj == h, den, jnp.where(kj == HEADS_B + h, scores[h][2], stats))
        lse_ref[rows, :] = stats


def _attn_group(q, kv, group, name, blocks_per_step=8):
    dilation = DILATIONS[group]
    blocks_per_seq = SEQ // dilation // BLOCK_B
    tq = blocks_per_step * BLOCK_B
    return pl.pallas_call(
        functools.partial(_attn_kernel, dilation=float(dilation), blocks_per_seq=blocks_per_seq),
        grid=(TOKENS // tq,),
        in_specs=[pl.BlockSpec((tq, INNER_B), lambda n: (n, 0)),
                  pl.BlockSpec((BLOCK_B, 2 * INNER_B),
                               lambda n: (jnp.maximum(n * blocks_per_step - 1, 0), 0)),
                  pl.BlockSpec((tq, 2 * INNER_B), lambda n: (n, 0))],
        out_specs=[pl.BlockSpec((tq, INNER_B), lambda n: (n, 0)),
                   pl.BlockSpec((tq, LANES), lambda n: (n, 0))],
        out_shape=[jax.ShapeDtypeStruct((TOKENS, INNER_B), BF16),
                   jax.ShapeDtypeStruct((TOKENS, LANES), F32)],
        compiler_params=pltpu.CompilerParams(dimension_semantics=("arbitrary",),
                                             vmem_limit_bytes=VMEM_LIMIT),
        name=name,
    )(q, kv, kv)


def _merge_out_ln_kernel(o0_ref, o1_ref, o2_ref, l0_ref, l1_ref, l2_ref, z_ref, w_ref, x_ref,
                         g_ref, b_ref, *rest, with_perm):
    n_out = 1 + (N_GROUPS_B if with_perm else 0)
    outs = rest[:n_out]
    y_ref, os1_ref, os2_ref, tmp_ref, ls1_ref, ls2_ref, ltmp_ref, wb_ref = rest[n_out:]
    tm = x_ref.shape[0]

    @pl.when(pl.program_id(0) == 0)
    def _():
        wb_ref[...] = w_ref[...].astype(BF16)

    q4, q16 = tm // RES_STEP, tm // (RES_STEP * RES_STEP)
    for r4 in range(RES_STEP):
        ls1_ref[pl.ds(r4, q4, stride=RES_STEP), :] = l1_ref[r4]
        for a in range(RES_STEP):
            ltmp_ref[pl.ds(r4 * q4 + a, q16, stride=RES_STEP), :] = l2_ref[r4 + RES_STEP * a]
        ls2_ref[pl.ds(r4, q4, stride=RES_STEP), :] = ltmp_ref[r4 * q4:(r4 + 1) * q4, :]
    for s in range(N_SLABS):
        lanes = slice(s * LANES, (s + 1) * LANES)
        for r4 in range(RES_STEP):
            os1_ref[s, pl.ds(r4, q4, stride=RES_STEP), :] = o1_ref[r4, :, lanes].astype(F32)
            for a in range(RES_STEP):
                tmp_ref[s, pl.ds(r4 * q4 + a, q16, stride=RES_STEP), :] = (
                    o2_ref[r4 + RES_STEP * a, :, lanes].astype(F32))
            os2_ref[s, pl.ds(r4, q4, stride=RES_STEP), :] = tmp_ref[s, r4 * q4:(r4 + 1) * q4, :]
    stats = (l0_ref[...], ls1_ref[...], ls2_ref[...])
    maxes = [pltpu.roll(t, LANES - HEADS_B, 1) for t in stats]
    head_lane = lax.broadcasted_iota(jnp.int32, (1, LANES), 1) < HEADS_B
    lses = [m + jnp.log(jnp.where(head_lane, t, 1.0)) for m, t in zip(maxes, stats)]
    mx = jnp.maximum(jnp.maximum(lses[0], lses[1]), lses[2])
    inv = 1.0 / (jnp.exp(lses[0] - mx) + jnp.exp(lses[1] - mx) + jnp.exp(lses[2] - mx))
    w0, w1, w2 = [jnp.exp(m - mx) * inv for m in maxes]
    for h in range(HEADS_B):
        sl = slice(h * HEAD_DIM_B, (h + 1) * HEAD_DIM_B)
        o = (w0[:, h:h + 1] * o0_ref[:, sl].astype(F32) + w1[:, h:h + 1] * os1_ref[h]
             + w2[:, h:h + 1] * os2_ref[h])
        zz = z_ref[:, sl].astype(F32)
        y_ref[:, sl] = (o * _silu(zz)).astype(BF16)
    r = ALPHA * x_ref[...] + _dot(y_ref[...], wb_ref[...])
    _store_stream(_layer_norm_rows(r, g_ref[...], b_ref[...]), outs[0], outs[1:], os1_ref, os2_ref)


def _merge_out_ln(os_, lses, z, w, layer, x, g, b, ln_layer, name, with_perm, tm=512):
    m = x.shape[0]
    n = D_MODEL
    per_seq = SEQ // tm
    tok = lambda width: pl.BlockSpec((tm, width), lambda i: (i, 0))
    res = lambda d, width: pl.BlockSpec((None, d, tm // d, width),
                                        lambda i: (i // per_seq, 0, i % per_seq, 0))
    vec = pl.BlockSpec((None, 1, n), lambda i: (ln_layer, 0, 0))
    d1, d2 = DILATIONS[1], DILATIONS[2]
    out_shape, out_specs = _stream_out(tm, with_perm)
    return pl.pallas_call(
        functools.partial(_merge_out_ln_kernel, with_perm=with_perm),
        grid=(m // tm,),
        in_specs=[tok(INNER_B), res(d1, INNER_B), res(d2, INNER_B),
                  tok(LANES), res(d1, LANES), res(d2, LANES),
                  tok(INNER_B),
                  pl.BlockSpec((None, INNER_B, n), lambda i: (layer, 0, 0)),
                  tok(n), vec, vec],
        out_specs=out_specs,
        out_shape=out_shape,
        scratch_shapes=[pltpu.VMEM((tm, INNER_B), BF16),
                        pltpu.VMEM((N_SLABS, tm, LANES), F32), pltpu.VMEM((N_SLABS, tm, LANES), F32),
                        pltpu.VMEM((N_SLABS, tm, LANES), F32),
                        pltpu.VMEM((tm, LANES), F32), pltpu.VMEM((tm, LANES), F32),
                        pltpu.VMEM((tm, LANES), F32), pltpu.VMEM((INNER_B, n), BF16)],
        compiler_params=pltpu.CompilerParams(dimension_semantics=("arbitrary",),
                                             vmem_limit_bytes=VMEM_LIMIT),
        name=name,
    )(os_[0], os_[1].reshape(BATCH, d1, SEQ // d1, INNER_B), os_[2].reshape(BATCH, d2, SEQ // d2, INNER_B),
      lses[0], lses[1].reshape(BATCH, d1, SEQ // d1, LANES), lses[2].reshape(BATCH, d2, SEQ // d2, LANES),
      z, w, x, g, b)


def _block_diag_lane_blocks(w):
    per = LANES // QKV_BLOCK
    w4 = w.reshape(-1, per, QKV_BLOCK, QKV_BLOCK)
    eye = jnp.eye(per, dtype=w.dtype)
    dense = jnp.einsum('bmij,mp->bmjpi', w4, eye)
    return dense.reshape(-1, LANES, LANES)


def _fold_gate_weights(wq, wk, wv, w_if):
    nblk = INNER_A // QKV_BLOCK
    wif = w_if.reshape(3, nblk, QKV_BLOCK, 2 * HEADS_A)
    hp = lax.Precision.HIGHEST
    wgc = (jnp.einsum('nij,nio->njo', wq, wif[0], precision=hp)
           + jnp.einsum('nij,nio->njo', wk, wif[1], precision=hp)).reshape(INNER_A, 2 * HEADS_A)
    wgm = jnp.einsum('nij,nio->njo', wv, wif[2], precision=hp).reshape(INNER_A, 2 * HEADS_A)
    pad = ((0, 0), (0, LANES - 2 * HEADS_A))
    return jnp.pad(wgc, pad).astype(BF16), jnp.pad(wgm, pad).astype(BF16)


def kernel(x, ln_g, ln_b, a_w_in, a_conv_w, a_conv_b, a_wq, a_wk, a_wv, a_w_if, a_b_if, a_gn_g, a_skip,
           a_w_out, b_w_kv, b_w_in, b_w_out):
    xs = x.reshape(TOKENS, D_MODEL)
    x_in = xs
    block_diag = jax.vmap(_block_diag_lane_blocks)
    bdqk = jnp.concatenate([block_diag(a_wq), block_diag(a_wk)], axis=-1).astype(BF16)
    bdv = block_diag(a_wv).astype(BF16)
    wgc, wgm = jax.vmap(_fold_gate_weights)(a_wq, a_wk, a_wv, a_w_if)
    bg = jnp.pad(a_b_if[:, None, :], ((0, 0), (0, 0), (0, LANES - 2 * HEADS_A)))
    conv_b, gn, skip = a_conv_b[:, None, :], a_gn_g[:, None, :], a_skip[:, None, :]
    ln_g3, ln_b3 = ln_g[:, None, :], ln_b[:, None, :]
    for layer in range(N_A_LAYERS):
        proj = _proj(x_in, a_w_in, layer, 0, 3 * INNER_A, f"a{layer}_proj")
        q, k, v, xc, wi, cv = _mlstm_pre(proj, layer, a_conv_w, conv_b, bdqk, bdv, wgc, wgm, bg,
                                         f"a{layer}_pre")
        y = _mlstm_core(q, k, v, xc, proj, wi, cv, layer, gn, skip, f"a{layer}_mlstm")
        outs = _out_ln(y, a_w_out, layer, xs, ln_g3, ln_b3, layer, f"a{layer}_out_ln",
                       with_perm=layer == N_A_LAYERS - 1)
        xs = outs[0]
        x_in = xs

    perms = [p.reshape(TOKENS, D_MODEL) for p in outs[1:]]
    kvs = [_proj(perms[g], b_w_kv[None], 0, 2 * g * INNER_B, 2 * INNER_B, f"kv_proj{g}")
           for g in range(N_GROUPS_B)]
    q_scale = HEAD_DIM_B ** -0.5
    for lb in range(N_B_LAYERS):
        layer = N_A_LAYERS + lb
        qs = [_proj(perms[g], b_w_in, lb, g * INNER_B, INNER_B, f"b{lb}_proj{g}", out_scale=q_scale)
              for g in range(N_GROUPS_B)]
        z = _proj(perms[0], b_w_in, lb, N_GROUPS_B * INNER_B, INNER_B, f"b{lb}_projz")
        os_, lses = [], []
        for g in range(N_GROUPS_B):
            o, lse = _attn_group(qs[g], kvs[g], g, f"b{lb}_attn{g}")
            os_.append(o)
            lses.append(lse)
        outs = _merge_out_ln(os_, lses, z, b_w_out, lb, xs, ln_g3, ln_b3, layer,
                             f"b{lb}_merge_out_ln", with_perm=lb < N_B_LAYERS - 1)
        xs = outs[0]
        perms = [p.reshape(TOKENS, D_MODEL) for p in outs[1:]]
    return xs.reshape(BATCH, SEQ, D_MODEL)
```
